```python
import math
import jax
import jax.numpy as jnp
from jax import lax
import numpy as np

D_MODEL = 1024
BATCH = 32
SEQ = 2048
DEPTH = 2
DEC_BATCH = 8
DEC_SEQ = 16
PAST_LEN = 4096

CHUNK = 64
GLA_BLOCK = 16
EPS = 1e-6

SSD_HEADS = 16
SSD_HEAD_DIM = 64
SSD_INNER = SSD_HEADS * SSD_HEAD_DIM
SSD_GROUPS = 2
SSD_STATE = 128
SSD_CONV = 4
SSD_CONV_DIM = SSD_INNER + 2 * SSD_GROUPS * SSD_STATE
GLA_HEADS = 4
GLA_DK = 128
GLA_DV = 256
GLA_RANK = 16
GLA_TAU = 16.0
ML_HEADS = 4
ML_DK = 256
ML_DV = 256
ML_INNER = ML_HEADS * ML_DV
FFN_HIDDEN = -(-8 * D_MODEL // (3 * 256)) * 256

IN_SPLITS = (SSD_INNER, SSD_CONV_DIM, SSD_HEADS,
             GLA_HEADS * GLA_DK, GLA_HEADS * GLA_DK, GLA_HEADS * GLA_DV,
             GLA_HEADS * GLA_DV, GLA_RANK,
             ML_HEADS * ML_DK, ML_HEADS * ML_DK, ML_INNER,
             ML_HEADS, ML_HEADS, ML_INNER,
             3 * D_MODEL)
IN_DIM = sum(IN_SPLITS)

kernel_name = 'gated_parallel_ssd_gla_mlstm_stream_step'


def _rmsnorm(x, g):
    xf = x.astype(jnp.float32)
    xf = xf * lax.rsqrt(jnp.mean(xf * xf, axis=-1, keepdims=True) + EPS)
    return xf.astype(x.dtype) * g


def _group_rmsnorm(x, groups, g):
    shp = x.shape
    xf = x.reshape(shp[:-1] + (groups, shp[-1] // groups)).astype(jnp.float32)
    xf = xf * lax.rsqrt(jnp.mean(xf * xf, axis=-1, keepdims=True) + EPS)
    return xf.reshape(shp).astype(x.dtype) * g


def _block_len(length, q):
    return q if length % q == 0 else length


def _to_blocks(a, q):
    b, length = a.shape[0], a.shape[1]
    return jnp.swapaxes(a.reshape((b, length // q, q) + a.shape[2:]), 0, 1)


def _from_blocks(a):
    a = jnp.swapaxes(a, 0, 1)
    return a.reshape((a.shape[0], a.shape[1] * a.shape[2]) + a.shape[3:])


def _causal_conv(u, buf, w, b):
    full = jnp.concatenate([buf.astype(u.dtype), u], axis=1)
    out = lax.conv_general_dilated(full, w[:, None, :].astype(u.dtype), (1,), 'VALID',
                                   dimension_numbers=('NWC', 'WIO', 'NWC'),
                                   feature_group_count=u.shape[-1])
    return jax.nn.silu(out + b), full[:, full.shape[1] - (SSD_CONV - 1):]


def _ssd_scan(xdt, da, bm, cm, s0):
    q = _block_len(xdt.shape[1], CHUNK)
    mask = jnp.tril(jnp.ones((q, q), dtype=bool))[None, :, :, None, None]

    def step(s, blk):
        xq, aq, bq, cq = blk
        acum = jnp.cumsum(aq, axis=1)
        decay = jnp.exp(jnp.where(mask, acum[:, :, None] - acum[:, None], -jnp.inf))
        w = jnp.einsum('btgn,bsgn->btsg', cq, bq)[..., None] * decay
        y = jnp.einsum('btsgh,bsghp->btghp', w, xq)
        y = y + jnp.exp(acum)[..., None] * jnp.einsum('btgn,bghpn->btghp', cq, s)
        tail = jnp.exp(acum[:, -1:] - acum)
        s_new = s * jnp.exp(acum[:, -1])[..., None, None] + jnp.einsum('bsghp,bsgn->bghpn', xq * tail[..., None], bq)
        return s_new, y

    s_fin, ys = lax.scan(step, s0, (_to_blocks(xdt, q), _to_blocks(da, q), _to_blocks(bm, q), _to_blocks(cm, q)))
    return _from_blocks(ys), s_fin


def _gla_scan(q, k, v, la, s0):
    blk = _block_len(q.shape[1], GLA_BLOCK)
    mask = jnp.tril(jnp.ones((blk, blk), dtype=bool))[None, :, :, None, None]

    def step(s, xs_):
        qb, kb, vb, lb = xs_
        bc = jnp.cumsum(lb, axis=1)
        decay = jnp.exp(jnp.where(mask, bc[:, :, None] - bc[:, None], -jnp.inf))
        att = jnp.sum(qb[:, :, None] * kb[:, None] * decay, axis=-1)
        o = jnp.einsum('btsh,bshv->bthv', att, vb) + jnp.einsum('bthk,bhkv->bthv', qb * jnp.exp(bc), s)
        kt = kb * jnp.exp(bc[:, -1:] - bc)
        s_new = s * jnp.exp(bc[:, -1])[..., None] + jnp.einsum('bshk,bshv->bhkv', kt, vb)
        return s_new, o

    s_fin, os_ = lax.scan(step, s0, (_to_blocks(q, blk), _to_blocks(k, blk), _to_blocks(v, blk), _to_blocks(la, blk)))
    return _from_blocks(os_), s_fin


def _mlstm_scan(q, k, v, ig, lf, c0, n0, m0):
    blk = _block_len(q.shape[1], CHUNK)
    mask = jnp.tril(jnp.ones((blk, blk), dtype=bool))[None, :, :, None]

    def step(carry, xs_):
        c, n, m = carry
        qb, kb, vb, ib, fb = xs_
        fc = jnp.cumsum(fb, axis=1)
        dmat = jnp.where(mask, fc[:, :, None] - fc[:, None] + ib[:, None], -jnp.inf)
        inter = fc + m[:, None]
        mt = jnp.maximum(inter, jnp.max(dmat, axis=2))
        w = jnp.einsum('bthk,bshk->btsh', qb, kb) * jnp.exp(dmat - mt[:, :, None])
        wi = jnp.exp(inter - mt)
        num = jnp.einsum('btsh,bshv->bthv', w, vb) + wi[..., None] * jnp.einsum('bthk,bhvk->bthv', qb, c)
        den = jnp.sum(w, axis=2) + wi * jnp.einsum('bthk,bhk->bth', qb, n)
        hb = num / jnp.maximum(jnp.abs(den), jnp.exp(-mt))[..., None]
        m_new = mt[:, -1]
        wt = jnp.exp(fc[:, -1:] - fc + ib - m_new[:, None])
        dc = jnp.exp(fc[:, -1] + m - m_new)
        c_new = c * dc[..., None, None] + jnp.einsum('bshv,bshk->bhvk', vb * wt[..., None], kb)
        n_new = n * dc[..., None] + jnp.einsum('bsh,bshk->bhk', wt, kb)
        return (c_new, n_new, m_new), hb

    (c_f, n_f, m_f), hs = lax.scan(step, (c0, n0, m0), (_to_blocks(q, blk), _to_blocks(k, blk), _to_blocks(v, blk),
                                                       _to_blocks(ig, blk), _to_blocks(lf, blk)))
    return _from_blocks(hs), (c_f, n_f, m_f)


def _mixers(h, st, p):
    conv0, ssd0, gla0, mc0, mn0, mm0 = st
    bsz, length = h.shape[0], h.shape[1]
    f32 = jnp.float32
    idx = np.cumsum(IN_SPLITS)[:-1].tolist()
    (z, xbc, dt_raw, gq, gk, gv, gr, glr, mq, mk, mv, mi, mf, mo, br) = jnp.split(h @ p['w_in'], idx, axis=-1)

    hg = SSD_HEADS // SSD_GROUPS
    xbc, conv_new = _causal_conv(xbc, conv0, p['ssd_conv_w'], p['ssd_conv_b'])
    xs, bm, cm = jnp.split(xbc, [SSD_INNER, SSD_INNER + SSD_GROUPS * SSD_STATE], axis=-1)
    xs = xs.reshape(bsz, length, SSD_GROUPS, hg, SSD_HEAD_DIM).astype(f32)
    bm = bm.reshape(bsz, length, SSD_GROUPS, SSD_STATE).astype(f32)
    cm = cm.reshape(bsz, length, SSD_GROUPS, SSD_STATE).astype(f32)
    dt = jax.nn.softplus((dt_raw + p['ssd_dt_bias']).astype(f32)).reshape(bsz, length, SSD_GROUPS, hg)
    a = -jnp.exp(p['ssd_a_log'].astype(f32)).reshape(SSD_GROUPS, hg)
    s0 = ssd0.astype(f32).reshape(bsz, SSD_GROUPS, hg, SSD_HEAD_DIM, SSD_STATE)
    y, ssd_new = _ssd_scan(xs * dt[..., None], dt * a, bm, cm, s0)
    y = y + xs * p['ssd_d'].astype(f32).reshape(SSD_GROUPS, hg, 1)
    y = y.reshape(bsz, length, SSD_INNER).astype(h.dtype) * jax.nn.silu(z)
    y_a = _group_rmsnorm(y, SSD_GROUPS, p['ssd_norm'])
    ssd_new = ssd_new.reshape(bsz, SSD_HEADS, SSD_HEAD_DIM, SSD_STATE)

    q = gq.reshape(bsz, length, GLA_HEADS, GLA_DK).astype(f32) * GLA_DK ** -0.5
    k = gk.reshape(bsz, length, GLA_HEADS, GLA_DK).astype(f32)
    v = gv.reshape(bsz, length, GLA_HEADS, GLA_DV).astype(f32)
    la = jax.nn.log_sigmoid((glr @ p['gla_w_gate'] + p['gla_b_gate']).astype(f32)) / GLA_TAU
    o, gla_new = _gla_scan(q, k, v, la.reshape(bsz, length, GLA_HEADS, GLA_DK), gla0.astype(f32))
    y_b = _group_rmsnorm(o.reshape(bsz, length, GLA_HEADS * GLA_DV).astype(h.dtype), GLA_HEADS, p['gla_norm']) * jax.nn.silu(gr)

    q = mq.reshape(bsz, length, ML_HEADS, ML_DK).astype(f32)
    k = mk.reshape(bsz, length, ML_HEADS, ML_DK).astype(f32) * ML_DK ** -0.5
    v = mv.reshape(bsz, length, ML_HEADS, ML_DV).astype(f32)
    ig = (mi + p['ml_b_i']).astype(f32)
    lf = jax.nn.log_sigmoid((mf + p['ml_b_f']).astype(f32))
    hm, (mc_new, mn_new, mm_new) = _mlstm_scan(q, k, v, ig, lf, mc0.astype(f32), mn0.astype(f32), mm0.astype(f32))
    hm = jax.nn.sigmoid(mo + p['ml_b_o']) * hm.reshape(bsz, length, ML_INNER).astype(h.dtype)
    y_c = _group_rmsnorm(hm, ML_HEADS, p['ml_norm'])

    g = jax.nn.sigmoid(br + p['b_branch']).reshape(bsz, length, 3, D_MODEL)
    merged = (g[:, :, 0] * (y_a @ p['w_br_ssd']) + g[:, :, 1] * (y_b @ p['w_br_gla'])
              + g[:, :, 2] * (y_c @ p['w_br_ml']))
    return merged @ p['w_out'], (conv_new, ssd_new, gla_new, mc_new, mn_new, mm_new)


def _layer(x, c, st, p):
    mod = jax.nn.silu(c) @ p['w_ada'] + p['b_ada']
    sh1, sc1, g1, sh2, sc2, g2 = [m[:, None, :] for m in jnp.split(mod, 6, axis=-1)]
    h = _rmsnorm(x, p['g_pre_mix']) * (1.0 + sc1) + sh1
    mix, st_new = _mixers(h, st, p)
    x = x + g1 * _rmsnorm(mix, p['g_post_mix'])
    h = _rmsnorm(x, p['g_pre_ffn']) * (1.0 + sc2) + sh2
    f = (jax.nn.silu(h @ p['w_ffn_gate']) * (h @ p['w_ffn_up'])) @ p['w_ffn_down']
    x = x + g2 * _rmsnorm(f, p['g_post_ffn'])
    return x, st_new


def _trunk(x, c, states, params):
    per_layer = []
    for l in range(DEPTH):
        p = {name: w[l] for name, w in params.items()}
        x, st = _layer(x, c, tuple(s[l] for s in states), p)
        per_layer.append(st)
    new = tuple(jnp.stack([st[i] for st in per_layer]) for i in range(6))
    return x, new


def setup_inputs(seed: int = 0) -> dict:
    key = jax.random.key(seed)
    ks = iter(jax.random.split(key, 48))
    f32 = jnp.float32

    def nrm(shape, s):
        return s * jax.random.normal(next(ks), shape, f32)

    D = D_MODEL
    x_prompt = nrm((BATCH, SEQ, D), 1.0)
    x_sample = nrm((DEC_BATCH, DEC_SEQ, D), 1.0)
    c_prompt = nrm((BATCH, D), 1.0)
    c_sample = nrm((DEC_BATCH, D), 1.0)
    state_ssd_conv = nrm((DEPTH, DEC_BATCH, SSD_CONV - 1, SSD_CONV_DIM), 1.0)
    state_ssd = nrm((DEPTH, DEC_BATCH, SSD_HEADS, SSD_HEAD_DIM, SSD_STATE), 0.5)
    state_gla = nrm((DEPTH, DEC_BATCH, GLA_HEADS, GLA_DK, GLA_DV), 0.5)
    state_mlstm_c = nrm((DEPTH, DEC_BATCH, ML_HEADS, ML_DV, ML_DK), 0.5)
    state_mlstm_n = nrm((DEPTH, DEC_BATCH, ML_HEADS, ML_DK), 0.5)
    state_mlstm_m = nrm((DEPTH, DEC_BATCH, ML_HEADS), 1.0)
    w_ada = nrm((DEPTH, D, 6 * D), D ** -0.5)
    b_ada = nrm((DEPTH, 6 * D), 0.02)
    g_pre_mix = 1.0 + nrm((DEPTH, D), 0.05)
    g_post_mix = 1.0 + nrm((DEPTH, D), 0.05)
    g_pre_ffn = 1.0 + nrm((DEPTH, D), 0.05)
    g_post_ffn = 1.0 + nrm((DEPTH, D), 0.05)
    w_in = nrm((DEPTH, D, IN_DIM), D ** -0.5)
    ssd_conv_w = nrm((DEPTH, SSD_CONV, SSD_CONV_DIM), SSD_CONV ** -0.5)
    ssd_conv_b = nrm((DEPTH, SSD_CONV_DIM), 0.02)
    dt0 = jnp.exp(jax.random.uniform(next(ks), (DEPTH, SSD_HEADS), f32, math.log(1e-3), math.log(1e-1)))
    ssd_dt_bias = dt0 + jnp.log(-jnp.expm1(-dt0))
    ssd_a_log = jnp.log(jax.random.uniform(next(ks), (DEPTH, SSD_HEADS), f32, 1.0, 16.0))
    ssd_d = 1.0 + nrm((DEPTH, SSD_HEADS), 0.1)
    ssd_norm = 1.0 + nrm((DEPTH, SSD_INNER), 0.05)
    gla_w_gate = nrm((DEPTH, GLA_RANK, GLA_HEADS * GLA_DK), GLA_RANK ** -0.5)
    gla_b_gate = nrm((DEPTH, GLA_HEADS * GLA_DK), 0.1)
    gla_norm = 1.0 + nrm((DEPTH, GLA_HEADS * GLA_DV), 0.05)
    ml_b_i = nrm((DEPTH, ML_HEADS), 0.1)
    ml_b_f = 3.0 + nrm((DEPTH, ML_HEADS), 0.5)
    ml_b_o = nrm((DEPTH, ML_INNER), 0.02)
    ml_norm = 1.0 + nrm((DEPTH, ML_INNER), 0.05)
    b_branch = nrm((DEPTH, 3 * D), 0.02)
    w_br_ssd = nrm((DEPTH, SSD_INNER, D), SSD_INNER ** -0.5)
    w_br_gla = nrm((DEPTH, GLA_HEADS * GLA_DV, D), (GLA_HEADS * GLA_DV) ** -0.5)
    w_br_ml = nrm((DEPTH, ML_INNER, D), ML_INNER ** -0.5)
    w_out = nrm((DEPTH, D, D), D ** -0.5)
    w_ffn_gate = nrm((DEPTH, D, FFN_HIDDEN), D ** -0.5)
    w_ffn_up = nrm((DEPTH, D, FFN_HIDDEN), D ** -0.5)
    w_ffn_down = nrm((DEPTH, FFN_HIDDEN, D), FFN_HIDDEN ** -0.5)
    return {'x_prompt': x_prompt, 'x_sample': x_sample, 'c_prompt': c_prompt, 'c_sample': c_sample,
            'state_ssd_conv': state_ssd_conv, 'state_ssd': state_ssd, 'state_gla': state_gla,
            'state_mlstm_c': state_mlstm_c, 'state_mlstm_n': state_mlstm_n, 'state_mlstm_m': state_mlstm_m,
            'w_ada': w_ada, 'b_ada': b_ada, 'g_pre_mix': g_pre_mix, 'g_post_mix': g_post_mix,
            'g_pre_ffn': g_pre_ffn, 'g_post_ffn': g_post_ffn, 'w_in': w_in,
            'ssd_conv_w': ssd_conv_w, 'ssd_conv_b': ssd_conv_b, 'ssd_dt_bias': ssd_dt_bias,
            'ssd_a_log': ssd_a_log, 'ssd_d': ssd_d, 'ssd_norm': ssd_norm,
            'gla_w_gate': gla_w_gate, 'gla_b_gate': gla_b_gate, 'gla_norm': gla_norm,
            'ml_b_i': ml_b_i, 'ml_b_f': ml_b_f, 'ml_b_o': ml_b_o, 'ml_norm': ml_norm,
            'b_branch': b_branch, 'w_br_ssd': w_br_ssd, 'w_br_gla': w_br_gla, 'w_br_ml': w_br_ml,
            'w_out': w_out, 'w_ffn_gate': w_ffn_gate, 'w_ffn_up': w_ffn_up, 'w_ffn_down': w_ffn_down}


def reference(x_prompt, x_sample, c_prompt, c_sample, state_ssd_conv, state_ssd, state_gla,
              state_mlstm_c, state_mlstm_n, state_mlstm_m, w_ada, b_ada, g_pre_mix, g_post_mix,
              g_pre_ffn, g_post_ffn, w_in, ssd_conv_w, ssd_conv_b, ssd_dt_bias, ssd_a_log, ssd_d,
              ssd_norm, gla_w_gate, gla_b_gate, gla_norm, ml_b_i, ml_b_f, ml_b_o, ml_norm,
              b_branch, w_br_ssd, w_br_gla, w_br_ml, w_out, w_ffn_gate, w_ffn_up, w_ffn_down):
    params = dict(w_ada=w_ada, b_ada=b_ada, g_pre_mix=g_pre_mix, g_post_mix=g_post_mix,
                  g_pre_ffn=g_pre_ffn, g_post_ffn=g_post_ffn, w_in=w_in, ssd_conv_w=ssd_conv_w,
                  ssd_conv_b=ssd_conv_b, ssd_dt_bias=ssd_dt_bias, ssd_a_log=ssd_a_log, ssd_d=ssd_d,
                  ssd_norm=ssd_norm, gla_w_gate=gla_w_gate, gla_b_gate=gla_b_gate, gla_norm=gla_norm,
                  ml_b_i=ml_b_i, ml_b_f=ml_b_f, ml_b_o=ml_b_o, ml_norm=ml_norm, b_branch=b_branch,
                  w_br_ssd=w_br_ssd, w_br_gla=w_br_gla, w_br_ml=w_br_ml, w_out=w_out,
                  w_ffn_gate=w_ffn_gate, w_ffn_up=w_ffn_up, w_ffn_down=w_ffn_down)
    b = x_prompt.shape[0]
    f32 = jnp.float32
    zero_states = (jnp.zeros((DEPTH, b, SSD_CONV - 1, SSD_CONV_DIM), x_prompt.dtype),
                   jnp.zeros((DEPTH, b, SSD_HEADS, SSD_HEAD_DIM, SSD_STATE), f32),
                   jnp.zeros((DEPTH, b, GLA_HEADS, GLA_DK, GLA_DV), f32),
                   jnp.zeros((DEPTH, b, ML_HEADS, ML_DV, ML_DK), f32),
                   jnp.zeros((DEPTH, b, ML_HEADS, ML_DK), f32),
                   jnp.zeros((DEPTH, b, ML_HEADS), f32))
    y_prompt, (p_conv, p_ssd, p_gla, p_mc, p_mn, p_mm) = _trunk(x_prompt, c_prompt, zero_states, params)
    sample_states = (state_ssd_conv, state_ssd, state_gla, state_mlstm_c, state_mlstm_n, state_mlstm_m)
    y_sample, (s_conv, s_ssd, s_gla, s_mc, s_mn, s_mm) = _trunk(x_sample, c_sample, sample_states, params)
    return (y_prompt, y_sample, p_conv, p_ssd, p_gla, p_mc, p_mn, p_mm, s_conv, s_ssd, s_gla, s_mc, s_mn, s_mm)
```

```python
import functools
import math

import numpy as np
import jax
import jax.numpy as jnp
from jax import lax
from jax.experimental import pallas as pl
from jax.experimental.pallas import tpu as pltpu

F32 = jnp.float32
BF16 = jnp.bfloat16

D_MODEL = 1024
EPS = 1e-6
SSD_HEADS = 16
SSD_HEAD_DIM = 64
SSD_INNER = 1024
SSD_GROUPS = 2
SSD_STATE = 128
SSD_CONV = 4
SSD_CONV_DIM = 1536
GLA_HEADS = 4
GLA_DK = 128
GLA_DV = 256
GLA_RANK = 16
GLA_TAU = 16.0
ML_HEADS = 4
ML_DK = 256
ML_DV = 256
ML_INNER = 1024
FFN_HIDDEN = 2816

_SPLITS = (1024, 1536, 16, 512, 512, 1024, 1024, 16, 1024, 1024, 1024, 4, 4, 1024, 3072)
_OFF = np.concatenate([[0], np.cumsum(_SPLITS)]).tolist()
(_O_Z, _O_XBC, _O_DT, _O_GQ, _O_GK, _O_GV, _O_GR, _O_GLR, _O_MQ, _O_MK, _O_MV, _O_MI, _O_MF, _O_MO,
 _O_BR, _O_END) = _OFF

C_Z, C_XBC, C_GQ, C_GK, C_GV, C_GR, C_MQ, C_MK, C_MV, C_MO, N_MAIN = (
    0, 1024, 2560, 3072, 3584, 4608, 5632, 6656, 7680, 8704, 9728)
S_DT, S_GLR, S_MI, S_MF = 0, 16, 32, 36
N_SMALL = 128

VMEM_LIMIT = 56 * 1024 * 1024


def _cparams(sem):
    return pltpu.CompilerParams(dimension_semantics=sem, vmem_limit_bytes=VMEM_LIMIT)


def _sigmoid(x):
    return 1.0 / (1.0 + jnp.exp(-x))


def _silu(x):
    return x * _sigmoid(x)


def _softplus_tail(x):
    return jnp.log1p(jnp.exp(-jnp.abs(x)))


def _nn(a, b):
    return jnp.dot(a, b, preferred_element_type=F32)


def _nt(a, b):
    return lax.dot_general(a, b, (((1,), (1,)), ((), ())), preferred_element_type=F32)


def _tn(a, b):
    return lax.dot_general(a, b, (((0,), (0,)), ((), ())), preferred_element_type=F32)


def _split3(x):
    hi = x.astype(BF16)
    r1 = x - hi.astype(F32)
    mid = r1.astype(BF16)
    r2 = r1 - mid.astype(F32)
    return hi, mid, r2.astype(BF16)


def _sel_left(w01, x, fn=_nn):
    hi, mid, lo = _split3(x)
    return fn(w01, hi) + fn(w01, mid) + fn(w01, lo)


def _sel_right(x, w01):
    hi, mid, lo = _split3(x)
    return _nn(hi, w01) + _nn(mid, w01) + _nn(lo, w01)


def _rms(x):
    return x * lax.rsqrt(jnp.mean(x * x, axis=-1, keepdims=True) + EPS)


def _group_rms(y, groups):
    n = y.shape[-1] // groups
    return jnp.concatenate([_rms(y[:, i * n:(i + 1) * n]) for i in range(groups)], axis=-1)


def _ada_kernel(c_ref, w_ref, b_ref, o_ref):
    c = c_ref[...]
    o_ref[0] = _nn(_silu(c).astype(BF16), w_ref[0]) + b_ref[0]


def _ada(c_all, w_ada, b_ada):
    depth = w_ada.shape[0]
    n = c_all.shape[0]
    d = D_MODEL
    return pl.pallas_call(
        _ada_kernel,
        grid=(depth, 6),
        in_specs=[pl.BlockSpec((n, d), lambda l, j: (0, 0)),
                  pl.BlockSpec((1, d, d), lambda l, j: (l, 0, j)),
                  pl.BlockSpec((1, 1, d), lambda l, j: (l, 0, j))],
        out_specs=pl.BlockSpec((1, n, d), lambda l, j: (l, 0, j)),
        out_shape=jax.ShapeDtypeStruct((depth, n, 6 * d), F32),
        compiler_params=_cparams(("parallel", "parallel")),
        name="ada",
    )(c_all, w_ada, b_ada.reshape(depth, 1, 6 * d))


def _prenorm(x, g, sh, sc):
    return _rms(x) * g * (1.0 + sc) + sh


def _inproj_kernel(x_ref, mod_ref, g_ref, w_ref, ws_ref, main_ref, small_ref, h_ref):
    tb, tl, d = x_ref.shape

    @pl.when(pl.program_id(2) == 0)
    def _():
        h = _prenorm(x_ref[...], g_ref[...], mod_ref[:, 0:1, :], mod_ref[:, 1:2, :])
        hb = h.reshape(tb * tl, d).astype(BF16)
        h_ref[...] = hb
        small_ref[...] = _nn(hb, ws_ref[...]).reshape(tb, tl, N_SMALL)

    main_ref[...] = _nn(h_ref[...], w_ref[...]).astype(BF16).reshape(main_ref.shape)


def _inproj(x, mod3, g, w_main, w_small, tb, tl, tn):
    b, l, d = x.shape
    grid = (b // tb, l // tl, N_MAIN // tn)
    return pl.pallas_call(
        _inproj_kernel,
        grid=grid,
        in_specs=[pl.BlockSpec((tb, tl, d), lambda i, k, j: (i, k, 0)),
                  pl.BlockSpec((tb, 6, d), lambda i, k, j: (i, 0, 0)),
                  pl.BlockSpec((1, 1, d), lambda i, k, j: (0, 0, 0)),
                  pl.BlockSpec((d, tn), lambda i, k, j: (0, j)),
                  pl.BlockSpec((d, N_SMALL), lambda i, k, j: (0, 0))],
        out_specs=[pl.BlockSpec((tb, tl, tn), lambda i, k, j: (i, k, j)),
                   pl.BlockSpec((tb, tl, N_SMALL), lambda i, k, j: (i, k, 0))],
        out_shape=[jax.ShapeDtypeStruct((b, l, N_MAIN), BF16),
                   jax.ShapeDtypeStruct((b, l, N_SMALL), F32)],
        scratch_shapes=[pltpu.VMEM((tb * tl, d), BF16)],
        compiler_params=_cparams(("parallel", "parallel", "arbitrary")),
        name="inproj",
    )(x, mod3, g.reshape(1, 1, d), w_main, w_small)


def _gla_level_constants(q):
    nlev = int(math.log2(q))
    assert 1 << nlev == q
    wall = np.zeros((nlev + 1, q, q), np.float32)
    masks = np.zeros((nlev + 1, q, q), np.float32)
    wall[0] = np.tril(np.ones((q, q), np.float32))
    for lev in range(nlev):
        seg = q >> lev
        half = seg // 2
        for t in range(q):
            pos = t % seg
            start = t - pos
            bnd = start + half - 1
            if pos >= half:
                wall[lev + 1, t, bnd + 1:t + 1] = 1.0
                masks[lev, t, start:start + half] = 1.0
            else:
                wall[lev + 1, t, t + 1:bnd + 1] = 1.0
    masks[nlev] = np.eye(q, dtype=np.float32)
    return nlev, wall.reshape((nlev + 1) * q, q), masks


def _mixer_kernel(main_ref, small_ref, conv0_ref, ssd0_ref, gla0_ref, mc0_ref, mn0_ref, mm0_ref,
                  convw_ref, convb_ref, rowp_ref, dx_ref, ssdnorm_ref, wgate_ref, bgate_ref, glanorm_ref,
                  mlbo_ref, mlnorm_ref, wall_ref, masks_ref,
                  y_ref, conv_ref, ssd_ref, gla_ref, mc_ref, mn_ref, mm_ref, ext_ref, *, q, nlev):
    @pl.when(pl.program_id(1) == 0)
    def _init():
        ext_ref[5:8, :] = conv0_ref[0]
        ssd_ref[...] = ssd0_ref[...]
        gla_ref[...] = gla0_ref[...]
        mc_ref[...] = mc0_ref[...]
        mn_ref[...] = mn0_ref[...]
        mm_ref[...] = mm0_ref[...]

    row = lax.broadcasted_iota(jnp.int32, (q, q), 0)
    col = lax.broadcasted_iota(jnp.int32, (q, q), 1)
    causal = row >= col
    lane = lax.broadcasted_iota(jnp.int32, (1, 128), 1)
    lane_lo = lane < 64
    row128 = lax.broadcasted_iota(jnp.int32, (128, 1), 0)
    neg_inf = -jnp.inf

    sm = small_ref[0]
    smb = sm + rowp_ref[0:1, :]
    tail = _softplus_tail(smb)
    sp = jnp.maximum(smb, 0.0) + tail
    lsg = jnp.minimum(smb, 0.0) - tail
    a_row = -jnp.exp(rowp_ref[1:2, :])
    pre = jnp.where(lane < S_GLR, sp * a_row,
                    jnp.where(lane >= S_MF, jnp.where(lane < S_MF + ML_HEADS, lsg, 0.0),
                              jnp.where(lane >= S_MI, smb, 0.0)))
    tril = wall_ref[0:q, :]
    cums = _sel_left(tril, pre)
    eye = (lax.broadcasted_iota(jnp.int32, (128, 128), 0)
           == lax.broadcasted_iota(jnp.int32, (128, 128), 1)).astype(BF16)
    tr = _sel_left(eye, jnp.concatenate([cums, pre], axis=0), _nt)

    ext_ref[8:8 + q, :] = main_ref[0, :, C_XBC:C_XBC + SSD_CONV_DIM].astype(F32)
    acc = convb_ref[...]
    for j in range(SSD_CONV):
        acc = acc + convw_ref[j:j + 1, :] * ext_ref[5 + j:5 + j + q, :]
    carry = ext_ref[5 + q:8 + q, :]
    ext_ref[5:8, :] = carry
    conv_ref[0] = carry
    xc = _silu(acc)
    xs = xc[:, :SSD_INNER]
    bm = xc[:, SSD_INNER:SSD_INNER + SSD_GROUPS * SSD_STATE]
    cm = xc[:, SSD_INNER + SSD_GROUPS * SSD_STATE:]

    expand = ((lax.broadcasted_iota(jnp.int32, (128, SSD_INNER), 1) >> 6)
              == lax.broadcasted_iota(jnp.int32, (128, SSD_INNER), 0)).astype(BF16)
    both = _sel_right(jnp.concatenate([sp, cums], axis=0), expand)
    dt_x = both[0:q]
    acum_x = both[q:2 * q]
    alast_x = acum_x[q - 1:q, :]
    xdt = xs * dt_x
    eac_x = jnp.exp(acum_x)
    xw = xdt * jnp.exp(alast_x - acum_x)
    y_parts = []
    for g in range(SSD_GROUPS):
        cg = cm[:, g * SSD_STATE:(g + 1) * SSD_STATE].astype(BF16)
        bg = bm[:, g * SSD_STATE:(g + 1) * SSD_STATE].astype(BF16)
        wg = _nt(cg, bg)
        for p in range(g * 4, g * 4 + 4):
            sl = slice(128 * p, 128 * p + 128)
            xp = xdt[:, sl]
            yp = None
            for j in range(2):
                h = 2 * p + j
                dec = jnp.exp(jnp.where(causal, cums[:, h:h + 1] - tr[h:h + 1, 0:q], neg_inf))
                a = (wg * dec).astype(BF16)
                xm = jnp.where(lane_lo if j == 0 else jnp.logical_not(lane_lo), xp, 0.0).astype(BF16)
                t = _nn(a, xm)
                yp = t if yp is None else yp + t
            s_old = ssd_ref[0, p]
            yp = yp + eac_x[:, sl] * _nt(cg, s_old.astype(BF16))
            e0 = jnp.exp(tr[2 * p:2 * p + 1, q - 1:q])
            e1 = jnp.exp(tr[2 * p + 1:2 * p + 2, q - 1:q])
            ssd_ref[0, p] = s_old * jnp.where(row128 < 64, e0, e1) + _tn(xw[:, sl].astype(BF16), bg)
            y_parts.append(yp)
    y = jnp.concatenate(y_parts, axis=-1) + xs * dx_ref[...]
    y = y * _silu(main_ref[0, :, C_Z:C_Z + SSD_INNER].astype(F32))
    y_ref[0, :, 0:SSD_INNER] = (_group_rms(y, SSD_GROUPS) * ssdnorm_ref[...]).astype(BF16)

    la_pre = _nn(sm.astype(BF16), wgate_ref[...]) + bgate_ref[...]
    la = (jnp.minimum(la_pre, 0.0) - _softplus_tail(la_pre)) * (1.0 / GLA_TAU)
    eall = _sel_left(wall_ref[...], la)
    o_parts = []
    for h in range(GLA_HEADS):
        ks = slice(GLA_DK * h, GLA_DK * (h + 1))
        qh = main_ref[0, :, C_GQ + GLA_DK * h:C_GQ + GLA_DK * (h + 1)].astype(F32) * (GLA_DK ** -0.5)
        kh = main_ref[0, :, C_GK + GLA_DK * h:C_GK + GLA_DK * (h + 1)].astype(F32)
        vh = main_ref[0, :, C_GV + GLA_DV * h:C_GV + GLA_DV * (h + 1)]
        att = _nt(qh.astype(BF16), kh.astype(BF16)) * masks_ref[nlev]
        for lev in range(nlev):
            ex = jnp.exp(eall[(lev + 1) * q:(lev + 2) * q, ks])
            att = att + _nt((qh * ex).astype(BF16), (kh * ex).astype(BF16)) * masks_ref[lev]
        bc = eall[0:q, ks]
        blast = bc[q - 1:q, :]
        st = gla_ref[0, h]
        o = _nn(att.astype(BF16), vh) + _nt((qh * jnp.exp(bc)).astype(BF16), st.astype(BF16))
        kt = (kh * jnp.exp(blast - bc)).astype(BF16)
        gla_ref[0, h] = st * jnp.exp(blast) + _tn(vh, kt)
        o_parts.append(_rms(o))
    gr = main_ref[0, :, C_GR:C_GR + GLA_HEADS * GLA_DV].astype(F32)
    y_ref[0, :, SSD_INNER:SSD_INNER + GLA_HEADS * GLA_DV] = (
        jnp.concatenate(o_parts, axis=-1) * glanorm_ref[...] * _silu(gr)).astype(BF16)

    mm_old = mm_ref[0]
    mm_new = jnp.zeros((1, 128), F32)
    h_parts = []
    for h in range(ML_HEADS):
        qh = main_ref[0, :, C_MQ + ML_DK * h:C_MQ + ML_DK * (h + 1)]
        kh = (main_ref[0, :, C_MK + ML_DK * h:C_MK + ML_DK * (h + 1)].astype(F32) * (ML_DK ** -0.5)).astype(BF16)
        vh = main_ref[0, :, C_MV + ML_DV * h:C_MV + ML_DV * (h + 1)]
        fcc = cums[:, S_MF + h:S_MF + h + 1]
        fcr = tr[S_MF + h:S_MF + h + 1, 0:q]
        igr = tr[S_MI + h:S_MI + h + 1, q:2 * q]
        igc = pre[:, S_MI + h:S_MI + h + 1]
        m0 = mm_old[:, h:h + 1]
        dmat = jnp.where(causal, fcc - fcr + igr, neg_inf)
        inter = fcc + m0
        mt = jnp.maximum(inter, jnp.max(dmat, axis=1, keepdims=True))
        w = _nt(qh, kh) * jnp.exp(dmat - mt)
        wi = jnp.exp(inter - mt)
        c_old = mc_ref[0, h]
        n_old = mn_ref[0, h:h + 1, :]
        num = _nn(w.astype(BF16), vh) + wi * _nt(qh, c_old.astype(BF16))
        den = jnp.sum(w, axis=1, keepdims=True) + wi * jnp.sum(qh.astype(F32) * n_old, axis=1, keepdims=True)
        hb = num / jnp.maximum(jnp.abs(den), jnp.exp(-mt))
        m_new = mt[q - 1:q, :]
        flast = fcc[q - 1:q, :]
        wt = jnp.exp(flast - fcc + igc - m_new)
        dc = jnp.exp(flast + m0 - m_new)
        mc_ref[0, h] = c_old * dc + _tn((vh.astype(F32) * wt).astype(BF16), kh)
        mn_ref[0, h:h + 1, :] = n_old * dc + jnp.sum(wt * kh.astype(F32), axis=0, keepdims=True)
        mm_new = jnp.where(lane == h, m_new, mm_new)
        og = _sigmoid(main_ref[0, :, C_MO + ML_DV * h:C_MO + ML_DV * (h + 1)].astype(F32)
                      + mlbo_ref[:, ML_DV * h:ML_DV * (h + 1)])
        h_parts.append(_rms(og * hb))
    mm_ref[0] = mm_new
    y_ref[0, :, 2 * D_MODEL:3 * D_MODEL] = (jnp.concatenate(h_parts, axis=-1) * mlnorm_ref[...]).astype(BF16)


def _mixers(main, small, st, lp, q):
    b, l, _ = main.shape
    conv0, ssd0, gla0t, mc0, mn0, mm0 = st
    nlev, wall, masks = _gla_level_constants(q)
    wall = jnp.asarray(wall, BF16)
    masks = jnp.asarray(masks, F32)
    per_b = lambda *shape: pl.BlockSpec((1,) + shape, lambda i, c: (i,) + (0,) * len(shape))
    const = lambda a: pl.BlockSpec(a.shape, lambda i, c: (0,) * a.ndim)
    consts = [lp['conv_w'], lp['conv_b'], lp['rowp'], lp['d_x'], lp['ssd_norm'], lp['w_gate'], lp['b_gate'],
              lp['gla_norm'], lp['ml_b_o'], lp['ml_norm'], wall, masks]
    state_specs = [per_b(3, SSD_CONV_DIM), per_b(8, 128, 128), per_b(GLA_HEADS, GLA_DV, GLA_DK),
                   per_b(ML_HEADS, ML_DV, ML_DK), per_b(ML_HEADS, ML_DK), per_b(1, 128)]
    state_shapes = [jax.ShapeDtypeStruct(s.shape, F32) for s in (conv0, ssd0, gla0t, mc0, mn0, mm0)]
    return pl.pallas_call(
        functools.partial(_mixer_kernel, q=q, nlev=nlev),
        grid=(b, l // q),
        in_specs=[pl.BlockSpec((1, q, N_MAIN), lambda i, c: (i, c, 0)),
                  pl.BlockSpec((1, q, N_SMALL), lambda i, c: (i, c, 0))] + state_specs + [const(a) for a in consts],
        out_specs=[pl.BlockSpec((1, q, 3 * D_MODEL), lambda i, c: (i, c, 0))] + state_specs,
        out_shape=[jax.ShapeDtypeStruct((b, l, 3 * D_MODEL), BF16)] + state_shapes,
        scratch_shapes=[pltpu.VMEM((q + 8, SSD_CONV_DIM), F32)],
        compiler_params=_cparams(("parallel", "arbitrary")),
        name="mixers",
    )(main, small, conv0, ssd0, gla0t, mc0, mn0, mm0, *consts)


def _merge_kernel(x_ref, y_ref, mod_ref, gpre_ref, gpost_ref, wbr_ref, bbr_ref, wabc_ref, wout_ref, o_ref):
    tb, tl, d = x_ref.shape
    x = x_ref[...]
    h = _prenorm(x, gpre_ref[...], mod_ref[:, 0:1, :], mod_ref[:, 1:2, :])
    hb = h.reshape(tb * tl, d).astype(BF16)
    y = y_ref[...].reshape(tb * tl, 3 * d)
    merged = None
    for i in range(3):
        gate = _sigmoid(_nn(hb, wbr_ref[:, i * d:(i + 1) * d]) + bbr_ref[:, i * d:(i + 1) * d])
        t = gate * _nn(y[:, i * d:(i + 1) * d], wabc_ref[i])
        merged = t if merged is None else merged + t
    mix = _nn(merged.astype(BF16), wout_ref[...])
    o_ref[...] = x + mod_ref[:, 2:3, :] * (_rms(mix) * gpost_ref[...]).reshape(tb, tl, d)


def _merge(x, y, mod3, lp, tb, tl):
    b, l, d = x.shape
    const = lambda a: pl.BlockSpec(a.shape, lambda i, k: (0,) * a.ndim)
    consts = [lp['g_pre_mix'], lp['g_post_mix'], lp['w_br'], lp['b_br'], lp['w_abc'], lp['w_out']]
    return pl.pallas_call(
        _merge_kernel,
        grid=(b // tb, l // tl),
        in_specs=[pl.BlockSpec((tb, tl, d), lambda i, k: (i, k, 0)),
                  pl.BlockSpec((tb, tl, 3 * d), lambda i, k: (i, k, 0)),
                  pl.BlockSpec((tb, 6, d), lambda i, k: (i, 0, 0))] + [const(a) for a in consts],
        out_specs=pl.BlockSpec((tb, tl, d), lambda i, k: (i, k, 0)),
        out_shape=jax.ShapeDtypeStruct((b, l, d), F32),
        compiler_params=_cparams(("parallel", "parallel")),
        name="merge",
    )(x, y, mod3, *consts)


def _ffn_kernel(x_ref, mod_ref, gpre_ref, gpost_ref, wg_ref, wu_ref, wd_ref, o_ref, h_ref, acc_ref):
    tb, tl, d = x_ref.shape
    j = pl.program_id(2)

    @pl.when(j == 0)
    def _():
        h = _prenorm(x_ref[...], gpre_ref[...], mod_ref[:, 3:4, :], mod_ref[:, 4:5, :])
        h_ref[...] = h.reshape(tb * tl, d).astype(BF16)
        acc_ref[...] = jnp.zeros_like(acc_ref)

    hb = h_ref[...]
    a = _nn(hb, wg_ref[...])
    u = _nn(hb, wu_ref[...])
    acc_ref[...] += _nn((_silu(a) * u).astype(BF16), wd_ref[...])

    @pl.when(j == pl.num_programs(2) - 1)
    def _():
        f = _rms(acc_ref[...]) * gpost_ref[...]
        o_ref[...] = x_ref[...] + mod_ref[:, 5:6, :] * f.reshape(tb, tl, d)


def _ffn(x, mod3, lp, tb, tl, th):
    b, l, d = x.shape
    const = lambda a: pl.BlockSpec(a.shape, lambda i, k, j: (0,) * a.ndim)
    return pl.pallas_call(
        _ffn_kernel,
        grid=(b // tb, l // tl, FFN_HIDDEN // th),
        in_specs=[pl.BlockSpec((tb, tl, d), lambda i, k, j: (i, k, 0)),
                  pl.BlockSpec((tb, 6, d), lambda i, k, j: (i, 0, 0)),
                  const(lp['g_pre_ffn']), const(lp['g_post_ffn']),
                  pl.BlockSpec((d, th), lambda i, k, j: (0, j)),
                  pl.BlockSpec((d, th), lambda i, k, j: (0, j)),
                  pl.BlockSpec((th, d), lambda i, k, j: (j, 0))],
        out_specs=pl.BlockSpec((tb, tl, d), lambda i, k, j: (i, k, 0)),
        out_shape=jax.ShapeDtypeStruct((b, l, d), F32),
        scratch_shapes=[pltpu.VMEM((tb * tl, d), BF16), pltpu.VMEM((tb * tl, d), F32)],
        compiler_params=_cparams(("parallel", "parallel", "arbitrary")),
        name="ffn",
    )(x, mod3, lp['g_pre_ffn'], lp['g_post_ffn'], lp['w_ffn_gate'], lp['w_ffn_up'], lp['w_ffn_down'])


def _pack_layer(p, l):
    d = D_MODEL
    w_in = p['w_in'][l]
    cut = lambda a, n: w_in[:, a:a + n]
    w_main = jnp.concatenate([cut(_O_Z, 1024), cut(_O_XBC, 1536), cut(_O_GQ, 512), cut(_O_GK, 512),
                              cut(_O_GV, 1024), cut(_O_GR, 1024), cut(_O_MQ, 1024), cut(_O_MK, 1024),
                              cut(_O_MV, 1024), cut(_O_MO, 1024)], axis=1).astype(BF16)
    w_small = jnp.concatenate([cut(_O_DT, 16), cut(_O_GLR, 16), cut(_O_MI, 4), cut(_O_MF, 4),
                               jnp.zeros((d, N_SMALL - 40), F32)], axis=1).astype(BF16)
    z = lambda n: jnp.zeros((n,), F32)
    bias_row = jnp.concatenate([p['ssd_dt_bias'][l], z(16), p['ml_b_i'][l], p['ml_b_f'][l], z(N_SMALL - 40)])
    alog_row = jnp.concatenate([p['ssd_a_log'][l], z(N_SMALL - 16)])
    rowp = jnp.zeros((8, N_SMALL), F32).at[0].set(bias_row).at[1].set(alog_row)
    w_gate = jnp.zeros((N_SMALL, GLA_HEADS * GLA_DK), F32).at[S_GLR:S_GLR + GLA_RANK].set(p['gla_w_gate'][l])
    row = lambda a: a.reshape(1, -1)
    return dict(
        w_main=w_main, w_small=w_small,
        w_br=cut(_O_BR, 3 * d).astype(BF16), b_br=row(p['b_branch'][l]),
        w_abc=jnp.stack([p['w_br_ssd'][l], p['w_br_gla'][l], p['w_br_ml'][l]]).astype(BF16),
        w_out=p['w_out'][l].astype(BF16),
        w_ffn_gate=p['w_ffn_gate'][l].astype(BF16), w_ffn_up=p['w_ffn_up'][l].astype(BF16),
        w_ffn_down=p['w_ffn_down'][l].astype(BF16),
        g_pre_mix=p['g_pre_mix'][l].reshape(1, 1, d), g_post_mix=row(p['g_post_mix'][l]),
        g_pre_ffn=p['g_pre_ffn'][l].reshape(1, 1, d), g_post_ffn=row(p['g_post_ffn'][l]),
        conv_w=p['ssd_conv_w'][l], conv_b=row(p['ssd_conv_b'][l]), rowp=rowp,
        d_x=row(jnp.repeat(p['ssd_d'][l], SSD_HEAD_DIM)), ssd_norm=row(p['ssd_norm'][l]),
        w_gate=w_gate.astype(BF16), b_gate=row(p['gla_b_gate'][l]), gla_norm=row(p['gla_norm'][l]),
        ml_b_o=row(p['ml_b_o'][l]), ml_norm=row(p['ml_norm'][l]))


def _tiles(b, l):
    if l >= 512:
        return 1, 512
    return b, l


def _trunk(x, mod, states, packed, q):
    b, l, d = x.shape
    depth = len(packed)
    tb, tl = _tiles(b, l)
    new = []
    for li in range(depth):
        lp = packed[li]
        mod3 = mod[li].reshape(b, 6, d)
        conv0, ssd0, gla0, mc0, mn0, mm0 = (s[li] for s in states)
        st = (conv0, ssd0.reshape(b, 8, 128, 128), jnp.swapaxes(gla0, -1, -2), mc0, mn0,
              jnp.pad(mm0, ((0, 0), (0, 128 - ML_HEADS))).reshape(b, 1, 128))
        main, small = _inproj(x, mod3, lp['g_pre_mix'], lp['w_main'], lp['w_small'], tb, tl, 512)
        y, conv_n, ssd_n, gla_n, mc_n, mn_n, mm_n = _mixers(main, small, st, lp, q)
        x = _merge(x, y, mod3, lp, tb, tl)
        x = _ffn(x, mod3, lp, tb, tl, FFN_HIDDEN // 2)
        new.append((conv_n, ssd_n.reshape(b, SSD_HEADS, SSD_HEAD_DIM, SSD_STATE), jnp.swapaxes(gla_n, -1, -2),
                    mc_n, mn_n, mm_n[:, 0, :ML_HEADS]))
    return x, tuple(jnp.stack([st[i] for st in new]) for i in range(6))


def kernel(x_prompt, x_sample, c_prompt, c_sample, state_ssd_conv, state_ssd, state_gla, state_mlstm_c, state_mlstm_n, state_mlstm_m, w_ada, b_ada, g_pre_mix, g_post_mix, g_pre_ffn, g_post_ffn, w_in, ssd_conv_w, ssd_conv_b, ssd_dt_bias, ssd_a_log, ssd_d, ssd_norm, gla_w_gate, gla_b_gate, gla_norm, ml_b_i, ml_b_f, ml_b_o, ml_norm, b_branch, w_br_ssd, w_br_gla, w_br_ml, w_out, w_ffn_gate, w_ffn_up, w_ffn_down):
    params = dict(g_pre_mix=g_pre_mix, g_post_mix=g_post_mix, g_pre_ffn=g_pre_ffn, g_post_ffn=g_post_ffn,
                  w_in=w_in, ssd_conv_w=ssd_conv_w, ssd_conv_b=ssd_conv_b, ssd_dt_bias=ssd_dt_bias,
                  ssd_a_log=ssd_a_log, ssd_d=ssd_d, ssd_norm=ssd_norm, gla_w_gate=gla_w_gate,
                  gla_b_gate=gla_b_gate, gla_norm=gla_norm, ml_b_i=ml_b_i, ml_b_f=ml_b_f, ml_b_o=ml_b_o,
                  ml_norm=ml_norm, b_branch=b_branch, w_br_ssd=w_br_ssd, w_br_gla=w_br_gla, w_br_ml=w_br_ml,
                  w_out=w_out, w_ffn_gate=w_ffn_gate, w_ffn_up=w_ffn_up, w_ffn_down=w_ffn_down)
    depth = w_in.shape[0]
    packed = [_pack_layer(params, l) for l in range(depth)]
    bp, lp_ = x_prompt.shape[0], x_prompt.shape[1]
    bs, ls = x_sample.shape[0], x_sample.shape[1]

    mod = _ada(jnp.concatenate([c_prompt, c_sample], axis=0), w_ada.astype(BF16), b_ada)
    mod_p, mod_s = mod[:, :bp], mod[:, bp:]

    zeros = lambda *shape: jnp.zeros((depth, bp) + shape, F32)
    zero_states = (zeros(SSD_CONV - 1, SSD_CONV_DIM), zeros(SSD_HEADS, SSD_HEAD_DIM, SSD_STATE),
                   zeros(GLA_HEADS, GLA_DK, GLA_DV), zeros(ML_HEADS, ML_DV, ML_DK), zeros(ML_HEADS, ML_DK),
                   zeros(ML_HEADS))
    q_p = 128 if lp_ % 128 == 0 else lp_
    y_p, st_p = _trunk(x_prompt, mod_p, zero_states, packed, q_p)
    sample_states = (state_ssd_conv, state_ssd, state_gla, state_mlstm_c, state_mlstm_n, state_mlstm_m)
    q_s = 128 if ls % 128 == 0 else ls
    y_s, st_s = _trunk(x_sample, mod_s, sample_states, packed, q_s)
    return (y_p, y_s) + st_p + st_s
```

```python
import functools
import math

import numpy as np
import jax
import jax.numpy as jnp
from jax import lax
from jax.experimental import pallas as pl
from jax.experimental.pallas import tpu as pltpu

F32 = jnp.float32
BF16 = jnp.bfloat16

D_MODEL = 1024
EPS = 1e-6
SSD_HEADS = 16
SSD_HEAD_DIM = 64
SSD_INNER = 1024
SSD_GROUPS = 2
SSD_STATE = 128
SSD_CONV = 4
SSD_CONV_DIM = 1536
GLA_HEADS = 4
GLA_DK = 128
GLA_DV = 256
GLA_RANK = 16
GLA_TAU = 16.0
ML_HEADS = 4
ML_DK = 256
ML_DV = 256
ML_INNER = 1024
FFN_HIDDEN = 2816

_SPLITS = (1024, 1536, 16, 512, 512, 1024, 1024, 16, 1024, 1024, 1024, 4, 4, 1024, 3072)
_OFF = np.concatenate([[0], np.cumsum(_SPLITS)]).tolist()
(_O_Z, _O_XBC, _O_DT, _O_GQ, _O_GK, _O_GV, _O_GR, _O_GLR, _O_MQ, _O_MK, _O_MV, _O_MI, _O_MF, _O_MO,
 _O_BR, _O_END) = _OFF

C_Z, C_XBC, C_GQ, C_GK, C_GV, C_GR, C_MQ, C_MK, C_MV, C_MO, N_MAIN = (
    0, 1024, 2560, 3072, 3584, 4608, 5632, 6656, 7680, 8704, 9728)
S_DT, S_GLR, S_MI, S_MF = 0, 16, 32, 36
N_SMALL = 128

VMEM_LIMIT = 56 * 1024 * 1024


def _cparams(sem):
    return pltpu.CompilerParams(dimension_semantics=sem, vmem_limit_bytes=VMEM_LIMIT)


def _sigmoid(x):
    return 1.0 / (1.0 + jnp.exp(-x))


def _silu(x):
    return x * _sigmoid(x)


def _softplus_tail(x):
    return jnp.log(1.0 + jnp.exp(-jnp.abs(x)))


def _nn(a, b):
    return jnp.dot(a, b, preferred_element_type=F32)


def _nt(a, b):
    return lax.dot_general(a, b, (((1,), (1,)), ((), ())), preferred_element_type=F32)


def _tn(a, b):
    return lax.dot_general(a, b, (((0,), (0,)), ((), ())), preferred_element_type=F32)


def _split(x, terms):
    parts = []
    for i in range(terms):
        p = x.astype(BF16)
        parts.append(p)
        if i + 1 < terms:
            x = x - p.astype(F32)
    return parts


def _sel_left(w01, x, fn=_nn, terms=3):
    parts = _split(x, terms)
    out = fn(w01, parts[0])
    for p in parts[1:]:
        out = out + fn(w01, p)
    return out


def _sel_right(x, w01, terms=3):
    parts = _split(x, terms)
    out = _nn(parts[0], w01)
    for p in parts[1:]:
        out = out + _nn(p, w01)
    return out


def _rms(x):
    return x * lax.rsqrt(jnp.mean(x * x, axis=-1, keepdims=True) + EPS)


def _group_rms(y, groups):
    n = y.shape[-1] // groups
    return jnp.concatenate([_rms(y[:, i * n:(i + 1) * n]) for i in range(groups)], axis=-1)


def _ada_kernel(c_ref, w_ref, b_ref, o_ref):
    c = c_ref[...]
    o_ref[0] = _nn(_silu(c).astype(BF16), w_ref[0]) + b_ref[0]


def _ada(c_all, w_ada, b_ada):
    depth = w_ada.shape[0]
    n = c_all.shape[0]
    d = D_MODEL
    return pl.pallas_call(
        _ada_kernel,
        grid=(depth, 6),
        in_specs=[pl.BlockSpec((n, d), lambda l, j: (0, 0)),
                  pl.BlockSpec((1, d, d), lambda l, j: (l, 0, j)),
                  pl.BlockSpec((1, 1, d), lambda l, j: (l, 0, j))],
        out_specs=pl.BlockSpec((1, n, d), lambda l, j: (l, 0, j)),
        out_shape=jax.ShapeDtypeStruct((depth, n, 6 * d), F32),
        compiler_params=_cparams(("parallel", "parallel")),
        name="ada",
    )(c_all, w_ada, b_ada.reshape(depth, 1, 6 * d))


def _prenorm(x, g, sh, sc):
    return _rms(x) * g * (1.0 + sc) + sh


def _gla_level_constants(q):
    nlev = int(math.log2(q))
    assert 1 << nlev == q
    wall = np.zeros((nlev + 1, q, q), np.float32)
    masks = np.zeros((nlev + 1, q, q), np.float32)
    wall[0] = np.tril(np.ones((q, q), np.float32))
    for lev in range(nlev):
        seg = q >> lev
        half = seg // 2
        for t in range(q):
            pos = t % seg
            start = t - pos
            bnd = start + half - 1
            if pos >= half:
                wall[lev + 1, t, bnd + 1:t + 1] = 1.0
                masks[lev, t, start:start + half] = 1.0
            else:
                wall[lev + 1, t, t + 1:bnd + 1] = 1.0
    masks[nlev] = np.eye(q, dtype=np.float32)
    return nlev, wall.reshape((nlev + 1) * q, q), masks


CONV_ROWS = 16
CARRY_ROWS = 128


def _conv_shift_constants(q):
    assert q >= CONV_ROWS
    taps = SSD_CONV - 1
    shift = np.zeros((taps * q, CARRY_ROWS + q), np.float32)
    for j in range(taps):
        d = taps - j
        for t in range(q):
            shift[j * q + t, CARRY_ROWS + t - d] = 1.0
            if t - d < 0:
                for term in (1, 2):
                    shift[j * q + t, CARRY_ROWS - term * CONV_ROWS + t - d] = 1.0
    return shift


PROJ_COLS = 512


def _project(x_ref, mod_ref, gpre_ref, wmain_ref, wsmall_ref, main_ref, sm_ref):
    hb = _prenorm(x_ref[...], gpre_ref[...], mod_ref[:, 0:1, :], mod_ref[:, 1:2, :])[0].astype(BF16)
    for j in range(N_MAIN // PROJ_COLS):
        cs = slice(j * PROJ_COLS, (j + 1) * PROJ_COLS)
        main_ref[:, cs] = _nn(hb, wmain_ref[:, cs]).astype(BF16)
    sm_ref[...] = _nn(hb, wsmall_ref[...])


def _mixer_kernel(x_ref, mod_ref, xn_ref, modn_ref, gpre_ref, wmain_ref, wsmall_ref,
                  conv0_ref, ssd0_ref, gla0_ref, mc0_ref, mn0_ref, mm0_ref,
                  convw_ref, convb_ref, rowp_ref, dx_ref, ssdnorm_ref, wgate_ref, bgate_ref, glanorm_ref,
                  mlbo_ref, mlnorm_ref, wall_ref, masks_ref, shift_ref,
                  y_ref, conv_ref, ssd_ref, gla_ref, mc_ref, mn_ref, mm_ref,
                  ext_ref, main0_ref, main1_ref, sm0_ref, sm1_ref, *, q, nlev):
    n = pl.program_id(0) * pl.num_programs(1) + pl.program_id(1)
    proj_w = (gpre_ref, wmain_ref, wsmall_ref)
    consts = (convw_ref, convb_ref, rowp_ref, dx_ref, ssdnorm_ref, wgate_ref, bgate_ref, glanorm_ref,
              mlbo_ref, mlnorm_ref, wall_ref, masks_ref, shift_ref)
    outs = (y_ref, conv_ref, ssd_ref, gla_ref, mc_ref, mn_ref, mm_ref, ext_ref)

    @pl.when(n == 0)
    def _first():
        _project(x_ref, mod_ref, *proj_w, main0_ref, sm0_ref)

    @pl.when(pl.program_id(1) > 0)
    def _carry():
        ext_ref[CARRY_ROWS - CONV_ROWS:CARRY_ROWS, :] = ext_ref[CARRY_ROWS + q - CONV_ROWS:CARRY_ROWS + q, :]
        ext_ref[CARRY_ROWS - 3 * CONV_ROWS:CARRY_ROWS - CONV_ROWS, :] = jnp.zeros(
            (2 * CONV_ROWS, SSD_CONV_DIM), BF16)

    @pl.when(pl.program_id(1) == 0)
    def _init():
        ext_ref[0:CARRY_ROWS, :] = jnp.zeros((CARRY_ROWS, SSD_CONV_DIM), BF16)
        for term, part in enumerate(_split(conv0_ref[0], 3)):
            ext_ref[CARRY_ROWS - (term + 1) * CONV_ROWS:CARRY_ROWS - term * CONV_ROWS, :] = part
        ssd_ref[...] = ssd0_ref[...]
        gla_ref[...] = gla0_ref[...]
        mc_ref[...] = mc0_ref[...]
        mn_ref[...] = mn0_ref[...]
        mm_ref[...] = mm0_ref[...]

    @pl.when(n % 2 == 0)
    def _even():
        _mix_chunk(main0_ref, sm0_ref, *consts, *outs, q=q, nlev=nlev)
        _project(xn_ref, modn_ref, *proj_w, main1_ref, sm1_ref)

    @pl.when(n % 2 == 1)
    def _odd():
        _mix_chunk(main1_ref, sm1_ref, *consts, *outs, q=q, nlev=nlev)
        _project(xn_ref, modn_ref, *proj_w, main0_ref, sm0_ref)


def _mix_chunk(main_ref, sm_ref,
               convw_ref, convb_ref, rowp_ref, dx_ref, ssdnorm_ref, wgate_ref, bgate_ref, glanorm_ref,
               mlbo_ref, mlnorm_ref, wall_ref, masks_ref, shift_ref,
               y_ref, conv_ref, ssd_ref, gla_ref, mc_ref, mn_ref, mm_ref, ext_ref, *, q, nlev):
    sm = sm_ref[...]
    row = lax.broadcasted_iota(jnp.int32, (q, q), 0)
    col = lax.broadcasted_iota(jnp.int32, (q, q), 1)
    causal = row >= col
    lane = lax.broadcasted_iota(jnp.int32, (1, 128), 1)
    lane_lo = lane < 64
    row128 = lax.broadcasted_iota(jnp.int32, (128, 1), 0)
    neg_inf = -jnp.inf

    smb = sm + rowp_ref[0:1, :]
    tail = _softplus_tail(smb)
    sp = jnp.maximum(smb, 0.0) + tail
    lsg = jnp.minimum(smb, 0.0) - tail
    a_row = -jnp.exp(rowp_ref[1:2, :])
    pre = jnp.where(lane < S_GLR, sp * a_row,
                    jnp.where(lane >= S_MF, jnp.where(lane < S_MF + ML_HEADS, lsg, 0.0),
                              jnp.where(lane >= S_MI, smb, 0.0)))
    tril = wall_ref[0:q, :]
    cums = _sel_left(tril, pre)
    eye = (lax.broadcasted_iota(jnp.int32, (128, 128), 0)
           == lax.broadcasted_iota(jnp.int32, (128, 128), 1)).astype(BF16)
    tr = _sel_left(eye, jnp.concatenate([cums, pre], axis=0), _nt)

    xraw = main_ref[:, C_XBC:C_XBC + SSD_CONV_DIM]
    ext_ref[CARRY_ROWS:CARRY_ROWS + q, :] = xraw
    ysh = _nn(shift_ref[...], ext_ref[...])
    acc = convb_ref[...] + convw_ref[SSD_CONV - 1:SSD_CONV, :] * xraw.astype(F32)
    for j in range(SSD_CONV - 1):
        acc = acc + convw_ref[j:j + 1, :] * ysh[j * q:(j + 1) * q]
    conv_ref[0] = main_ref[q - CONV_ROWS:q, C_XBC:C_XBC + SSD_CONV_DIM].astype(F32)
    xc = _silu(acc)
    xs = xc[:, :SSD_INNER]
    bm = xc[:, SSD_INNER:SSD_INNER + SSD_GROUPS * SSD_STATE]
    cm = xc[:, SSD_INNER + SSD_GROUPS * SSD_STATE:]

    expand = ((lax.broadcasted_iota(jnp.int32, (128, SSD_INNER), 1) >> 6)
              == lax.broadcasted_iota(jnp.int32, (128, SSD_INNER), 0)).astype(BF16)
    is_dt = lane < S_GLR
    acum = jnp.where(is_dt, cums, 0.0)
    dt = jnp.where(is_dt, sp, 0.0)
    eac = jnp.exp(acum)
    dtt = dt * jnp.exp(acum[q - 1:q, :] - acum)
    ex3 = _sel_right(jnp.concatenate([dt, eac, dtt], axis=0), expand, terms=2)
    xdt = xs * ex3[0:q]
    eac_x = ex3[q:2 * q]
    xw = xs * ex3[2 * q:3 * q]
    y_parts = []
    for g in range(SSD_GROUPS):
        cg = cm[:, g * SSD_STATE:(g + 1) * SSD_STATE].astype(BF16)
        bg = bm[:, g * SSD_STATE:(g + 1) * SSD_STATE].astype(BF16)
        wg = _nt(cg, bg)
        for p in range(g * 4, g * 4 + 4):
            sl = slice(128 * p, 128 * p + 128)
            xp = xdt[:, sl]
            a2, x2 = [], []
            for j in range(2):
                h = 2 * p + j
                dec = jnp.exp(jnp.where(causal, cums[:, h:h + 1] - tr[h:h + 1, 0:q], neg_inf))
                a2.append((wg * dec).astype(BF16))
                x2.append(jnp.where(lane_lo if j == 0 else jnp.logical_not(lane_lo), xp, 0.0).astype(BF16))
            yp = _nn(jnp.concatenate(a2, axis=1), jnp.concatenate(x2, axis=0))
            s_old = ssd_ref[0, p]
            yp = yp + eac_x[:, sl] * _nt(cg, s_old.astype(BF16))
            e0 = jnp.exp(tr[2 * p:2 * p + 1, q - 1:q])
            e1 = jnp.exp(tr[2 * p + 1:2 * p + 2, q - 1:q])
            ssd_ref[0, p] = s_old * jnp.where(row128 < 64, e0, e1) + _tn(xw[:, sl].astype(BF16), bg)
            y_parts.append(yp)
    y = jnp.concatenate(y_parts, axis=-1) + xs * dx_ref[...]
    y = y * _silu(main_ref[:, C_Z:C_Z + SSD_INNER].astype(F32))
    y_ref[0, :, 0:SSD_INNER] = (_group_rms(y, SSD_GROUPS) * ssdnorm_ref[...]).astype(BF16)

    la_pre = _nn(sm.astype(BF16), wgate_ref[...]) + bgate_ref[...]
    la = (jnp.minimum(la_pre, 0.0) - _softplus_tail(la_pre)) * (1.0 / GLA_TAU)
    eall = _sel_left(wall_ref[...], la, terms=2)
    o_parts = []
    for h in range(GLA_HEADS):
        ks = slice(GLA_DK * h, GLA_DK * (h + 1))
        qh = main_ref[:, C_GQ + GLA_DK * h:C_GQ + GLA_DK * (h + 1)].astype(F32) * (GLA_DK ** -0.5)
        kh = main_ref[:, C_GK + GLA_DK * h:C_GK + GLA_DK * (h + 1)].astype(F32)
        vh = main_ref[:, C_GV + GLA_DV * h:C_GV + GLA_DV * (h + 1)]
        att = _nt(qh.astype(BF16), kh.astype(BF16)) * masks_ref[nlev]
        for lev in range(nlev):
            ex = jnp.exp(eall[(lev + 1) * q:(lev + 2) * q, ks])
            att = att + _nt((qh * ex).astype(BF16), (kh * ex).astype(BF16)) * masks_ref[lev]
        bc = eall[0:q, ks]
        blast = bc[q - 1:q, :]
        st = gla_ref[0, h]
        o = _nn(att.astype(BF16), vh) + _nt((qh * jnp.exp(bc)).astype(BF16), st.astype(BF16))
        kt = (kh * jnp.exp(blast - bc)).astype(BF16)
        gla_ref[0, h] = st * jnp.exp(blast) + _tn(vh, kt)
        o_parts.append(_rms(o))
    gr = main_ref[:, C_GR:C_GR + GLA_HEADS * GLA_DV].astype(F32)
    y_ref[0, :, SSD_INNER:SSD_INNER + GLA_HEADS * GLA_DV] = (
        jnp.concatenate(o_parts, axis=-1) * glanorm_ref[...] * _silu(gr)).astype(BF16)

    mm_old = mm_ref[0]
    mm_new = jnp.zeros((1, 128), F32)
    h_parts = []
    for h in range(ML_HEADS):
        qh = main_ref[:, C_MQ + ML_DK * h:C_MQ + ML_DK * (h + 1)]
        kh = (main_ref[:, C_MK + ML_DK * h:C_MK + ML_DK * (h + 1)].astype(F32) * (ML_DK ** -0.5)).astype(BF16)
        vh = main_ref[:, C_MV + ML_DV * h:C_MV + ML_DV * (h + 1)]
        fcc = cums[:, S_MF + h:S_MF + h + 1]
        fcr = tr[S_MF + h:S_MF + h + 1, 0:q]
        igr = tr[S_MI + h:S_MI + h + 1, q:2 * q]
        igc = pre[:, S_MI + h:S_MI + h + 1]
        m0 = mm_old[:, h:h + 1]
        dmat = jnp.where(causal, fcc - fcr + igr, neg_inf)
        inter = fcc + m0
        mt = jnp.maximum(inter, jnp.max(dmat, axis=1, keepdims=True))
        w = _nt(qh, kh) * jnp.exp(dmat - mt)
        wi = jnp.exp(inter - mt)
        c_old = mc_ref[0, h]
        n_old = mn_ref[0, h:h + 1, :]
        num = _nn(w.astype(BF16), vh) + wi * _nt(qh, c_old.astype(BF16))
        den = jnp.sum(w, axis=1, keepdims=True) + wi * jnp.sum(qh.astype(F32) * n_old, axis=1, keepdims=True)
        hb = num / jnp.maximum(jnp.abs(den), jnp.exp(-mt))
        m_new = mt[q - 1:q, :]
        flast = fcc[q - 1:q, :]
        wt = jnp.exp(flast - fcc + igc - m_new)
        dc = jnp.exp(flast + m0 - m_new)
        mc_ref[0, h] = c_old * dc + _tn((vh.astype(F32) * wt).astype(BF16), kh)
        mn_ref[0, h:h + 1, :] = n_old * dc + jnp.sum(wt * kh.astype(F32), axis=0, keepdims=True)
        mm_new = jnp.where(lane == h, m_new, mm_new)
        og = _sigmoid(main_ref[:, C_MO + ML_DV * h:C_MO + ML_DV * (h + 1)].astype(F32)
                      + mlbo_ref[:, ML_DV * h:ML_DV * (h + 1)])
        h_parts.append(_rms(og * hb))
    mm_ref[0] = mm_new
    y_ref[0, :, 2 * D_MODEL:3 * D_MODEL] = (jnp.concatenate(h_parts, axis=-1) * mlnorm_ref[...]).astype(BF16)


def _mixers(x, mod3, st, lp, q):
    b, l, d = x.shape
    conv0, ssd0, gla0t, mc0, mn0, mm0 = st
    resident = lambda a: pl.BlockSpec(a.shape, lambda i, c: (0,) * a.ndim, pipeline_mode=pl.Buffered(1))
    n_chunks = l // q

    def nxt(i, c):
        n = jnp.minimum(i * n_chunks + c + 1, b * n_chunks - 1)
        return n // n_chunks, n % n_chunks
    nlev, wall, masks = _gla_level_constants(q)
    wall = jnp.asarray(wall, BF16)
    masks = jnp.asarray(masks, F32)
    shift = _conv_shift_constants(q)
    per_b = lambda *shape: pl.BlockSpec((1,) + shape, lambda i, c: (i,) + (0,) * len(shape))
    const = lambda a: pl.BlockSpec(a.shape, lambda i, c: (0,) * a.ndim)
    consts = [lp['conv_w'], lp['conv_b'], lp['rowp'], lp['d_x'], lp['ssd_norm'], lp['w_gate'], lp['b_gate'],
              lp['gla_norm'], lp['ml_b_o'], lp['ml_norm'], wall, masks,
              jnp.asarray(shift, BF16)]
    state_specs = [per_b(CONV_ROWS, SSD_CONV_DIM), per_b(8, 128, 128), per_b(GLA_HEADS, GLA_DV, GLA_DK),
                   per_b(ML_HEADS, ML_DV, ML_DK), per_b(ML_HEADS, ML_DK), per_b(1, 128)]
    state_shapes = [jax.ShapeDtypeStruct(s.shape, F32) for s in (conv0, ssd0, gla0t, mc0, mn0, mm0)]
    return pl.pallas_call(
        functools.partial(_mixer_kernel, q=q, nlev=nlev),
        grid=(b, l // q),
        in_specs=[pl.BlockSpec((1, q, d), lambda i, c: (i, c, 0)),
                  pl.BlockSpec((1, 6, d), lambda i, c: (i, 0, 0)),
                  pl.BlockSpec((1, q, d), lambda i, c: nxt(i, c) + (0,)),
                  pl.BlockSpec((1, 6, d), lambda i, c: (nxt(i, c)[0], 0, 0)),
                  const(lp['g_pre_mix']), resident(lp['w_main']), resident(lp['w_small'])]
        + state_specs + [const(a) for a in consts],
        out_specs=[pl.BlockSpec((1, q, 3 * D_MODEL), lambda i, c: (i, c, 0))] + state_specs,
        out_shape=[jax.ShapeDtypeStruct((b, l, 3 * D_MODEL), BF16)] + state_shapes,
        scratch_shapes=[pltpu.VMEM((CARRY_ROWS + q, SSD_CONV_DIM), BF16),
                        pltpu.VMEM((q, N_MAIN), BF16), pltpu.VMEM((q, N_MAIN), BF16),
                        pltpu.VMEM((q, N_SMALL), F32), pltpu.VMEM((q, N_SMALL), F32)],
        compiler_params=_cparams(("arbitrary", "arbitrary")),
        name="mixers",
    )(x, mod3, x, mod3, lp['g_pre_mix'], lp['w_main'], lp['w_small'], conv0, ssd0, gla0t, mc0, mn0, mm0, *consts)


def _merge_kernel(x_ref, y_ref, mod_ref, gpre_ref, gpost_ref, wbr_ref, bbr_ref, wabc_ref, wout_ref, o_ref):
    tb, tl, d = x_ref.shape
    x = x_ref[...]
    h = _prenorm(x, gpre_ref[...], mod_ref[:, 0:1, :], mod_ref[:, 1:2, :])
    hb = h.reshape(tb * tl, d).astype(BF16)
    y = y_ref[...].reshape(tb * tl, 3 * d)
    merged = None
    for i in range(3):
        gate = _sigmoid(_nn(hb, wbr_ref[:, i * d:(i + 1) * d]) + bbr_ref[:, i * d:(i + 1) * d])
        t = gate * _nn(y[:, i * d:(i + 1) * d], wabc_ref[i])
        merged = t if merged is None else merged + t
    mix = _nn(merged.astype(BF16), wout_ref[...])
    o_ref[...] = x + mod_ref[:, 2:3, :] * (_rms(mix) * gpost_ref[...]).reshape(tb, tl, d)


def _merge(x, y, mod3, lp, tb, tl):
    b, l, d = x.shape
    const = lambda a: pl.BlockSpec(a.shape, lambda i, k: (0,) * a.ndim)
    consts = [lp['g_pre_mix'], lp['g_post_mix'], lp['w_br'], lp['b_br'], lp['w_abc'], lp['w_out']]
    return pl.pallas_call(
        _merge_kernel,
        grid=(b // tb, l // tl),
        in_specs=[pl.BlockSpec((tb, tl, d), lambda i, k: (i, k, 0)),
                  pl.BlockSpec((tb, tl, 3 * d), lambda i, k: (i, k, 0)),
                  pl.BlockSpec((tb, 6, d), lambda i, k: (i, 0, 0))] + [const(a) for a in consts],
        out_specs=pl.BlockSpec((tb, tl, d), lambda i, k: (i, k, 0)),
        out_shape=jax.ShapeDtypeStruct((b, l, d), F32),
        compiler_params=_cparams(("parallel", "parallel")),
        name="merge",
    )(x, y, mod3, *consts)


def _ffn_kernel(x_ref, mod_ref, gpre_ref, gpost_ref, wg_ref, wu_ref, wd_ref, o_ref, h_ref, acc_ref):
    tb, tl, d = x_ref.shape
    j = pl.program_id(2)

    @pl.when(j == 0)
    def _():
        h = _prenorm(x_ref[...], gpre_ref[...], mod_ref[:, 3:4, :], mod_ref[:, 4:5, :])
        h_ref[...] = h.reshape(tb * tl, d).astype(BF16)
        acc_ref[...] = jnp.zeros_like(acc_ref)

    hb = h_ref[...]
    a = _nn(hb, wg_ref[...])
    u = _nn(hb, wu_ref[...])
    acc_ref[...] += _nn((_silu(a) * u).astype(BF16), wd_ref[...])

    @pl.when(j == pl.num_programs(2) - 1)
    def _():
        f = _rms(acc_ref[...]) * gpost_ref[...]
        o_ref[...] = x_ref[...] + mod_ref[:, 5:6, :] * f.reshape(tb, tl, d)


def _ffn(x, mod3, lp, tb, tl, th):
    b, l, d = x.shape
    const = lambda a: pl.BlockSpec(a.shape, lambda i, k, j: (0,) * a.ndim)
    return pl.pallas_call(
        _ffn_kernel,
        grid=(b // tb, l // tl, FFN_HIDDEN // th),
        in_specs=[pl.BlockSpec((tb, tl, d), lambda i, k, j: (i, k, 0)),
                  pl.BlockSpec((tb, 6, d), lambda i, k, j: (i, 0, 0)),
                  const(lp['g_pre_ffn']), const(lp['g_post_ffn']),
                  pl.BlockSpec((d, th), lambda i, k, j: (0, j)),
                  pl.BlockSpec((d, th), lambda i, k, j: (0, j)),
                  pl.BlockSpec((th, d), lambda i, k, j: (j, 0))],
        out_specs=pl.BlockSpec((tb, tl, d), lambda i, k, j: (i, k, 0)),
        out_shape=jax.ShapeDtypeStruct((b, l, d), F32),
        scratch_shapes=[pltpu.VMEM((tb * tl, d), BF16), pltpu.VMEM((tb * tl, d), F32)],
        compiler_params=_cparams(("parallel", "parallel", "arbitrary")),
        name="ffn",
    )(x, mod3, lp['g_pre_ffn'], lp['g_post_ffn'], lp['w_ffn_gate'], lp['w_ffn_up'], lp['w_ffn_down'])


def _pack_layer(p, l):
    d = D_MODEL
    w_in = p['w_in'][l]
    cut = lambda a, n: w_in[:, a:a + n]
    w_main = jnp.concatenate([cut(_O_Z, 1024), cut(_O_XBC, 1536), cut(_O_GQ, 512), cut(_O_GK, 512),
                              cut(_O_GV, 1024), cut(_O_GR, 1024), cut(_O_MQ, 1024), cut(_O_MK, 1024),
                              cut(_O_MV, 1024), cut(_O_MO, 1024)], axis=1).astype(BF16)
    w_small = jnp.concatenate([cut(_O_DT, 16), cut(_O_GLR, 16), cut(_O_MI, 4), cut(_O_MF, 4),
                               jnp.zeros((d, N_SMALL - 40), F32)], axis=1).astype(BF16)
    z = lambda n: jnp.zeros((n,), F32)
    bias_row = jnp.concatenate([p['ssd_dt_bias'][l], z(16), p['ml_b_i'][l], p['ml_b_f'][l], z(N_SMALL - 40)])
    alog_row = jnp.concatenate([p['ssd_a_log'][l], z(N_SMALL - 16)])
    rowp = jnp.zeros((8, N_SMALL), F32).at[0].set(bias_row).at[1].set(alog_row)
    w_gate = jnp.zeros((N_SMALL, GLA_HEADS * GLA_DK), F32).at[S_GLR:S_GLR + GLA_RANK].set(p['gla_w_gate'][l])
    row = lambda a: a.reshape(1, -1)
    return dict(
        w_main=w_main, w_small=w_small,
        w_br=cut(_O_BR, 3 * d).astype(BF16), b_br=row(p['b_branch'][l]),
        w_abc=jnp.stack([p['w_br_ssd'][l], p['w_br_gla'][l], p['w_br_ml'][l]]).astype(BF16),
        w_out=p['w_out'][l].astype(BF16),
        w_ffn_gate=p['w_ffn_gate'][l].astype(BF16), w_ffn_up=p['w_ffn_up'][l].astype(BF16),
        w_ffn_down=p['w_ffn_down'][l].astype(BF16),
        g_pre_mix=p['g_pre_mix'][l].reshape(1, 1, d), g_post_mix=row(p['g_post_mix'][l]),
        g_pre_ffn=p['g_pre_ffn'][l].reshape(1, 1, d), g_post_ffn=row(p['g_post_ffn'][l]),
        conv_w=p['ssd_conv_w'][l], conv_b=row(p['ssd_conv_b'][l]), rowp=rowp,
        d_x=row(jnp.repeat(p['ssd_d'][l], SSD_HEAD_DIM)), ssd_norm=row(p['ssd_norm'][l]),
        w_gate=w_gate.astype(BF16), b_gate=row(p['gla_b_gate'][l]), gla_norm=row(p['gla_norm'][l]),
        ml_b_o=row(p['ml_b_o'][l]), ml_norm=row(p['ml_norm'][l]))


def _tiles(b, l):
    if l >= 512:
        return 1, 512
    return b, l


def _trunk(x, mod, states, packed, q):
    b, l, d = x.shape
    depth = len(packed)
    tb, tl = _tiles(b, l)
    new = []
    for li in range(depth):
        lp = packed[li]
        mod3 = mod[li].reshape(b, 6, d)
        conv0, ssd0, gla0, mc0, mn0, mm0 = (s[li] for s in states)
        st = (jnp.pad(conv0, ((0, 0), (CONV_ROWS - (SSD_CONV - 1), 0), (0, 0))),
              ssd0.reshape(b, 8, 128, 128), jnp.swapaxes(gla0, -1, -2), mc0, mn0,
              jnp.pad(mm0, ((0, 0), (0, 128 - ML_HEADS))).reshape(b, 1, 128))
        y, conv_n, ssd_n, gla_n, mc_n, mn_n, mm_n = _mixers(x, mod3, st, lp, q)
        x = _merge(x, y, mod3, lp, tb, tl)
        x = _ffn(x, mod3, lp, tb, tl, FFN_HIDDEN // 2)
        new.append((conv_n[:, CONV_ROWS - (SSD_CONV - 1):], ssd_n.reshape(b, SSD_HEADS, SSD_HEAD_DIM, SSD_STATE), jnp.swapaxes(gla_n, -1, -2),
                    mc_n, mn_n, mm_n[:, 0, :ML_HEADS]))
    return x, tuple(jnp.stack([st[i] for st in new]) for i in range(6))


def kernel(x_prompt, x_sample, c_prompt, c_sample, state_ssd_conv, state_ssd, state_gla, state_mlstm_c, state_mlstm_n, state_mlstm_m, w_ada, b_ada, g_pre_mix, g_post_mix, g_pre_ffn, g_post_ffn, w_in, ssd_conv_w, ssd_conv_b, ssd_dt_bias, ssd_a_log, ssd_d, ssd_norm, gla_w_gate, gla_b_gate, gla_norm, ml_b_i, ml_b_f, ml_b_o, ml_norm, b_branch, w_br_ssd, w_br_gla, w_br_ml, w_out, w_ffn_gate, w_ffn_up, w_ffn_down):
    params = dict(g_pre_mix=g_pre_mix, g_post_mix=g_post_mix, g_pre_ffn=g_pre_ffn, g_post_ffn=g_post_ffn,
                  w_in=w_in, ssd_conv_w=ssd_conv_w, ssd_conv_b=ssd_conv_b, ssd_dt_bias=ssd_dt_bias,
                  ssd_a_log=ssd_a_log, ssd_d=ssd_d, ssd_norm=ssd_norm, gla_w_gate=gla_w_gate,
                  gla_b_gate=gla_b_gate, gla_norm=gla_norm, ml_b_i=ml_b_i, ml_b_f=ml_b_f, ml_b_o=ml_b_o,
                  ml_norm=ml_norm, b_branch=b_branch, w_br_ssd=w_br_ssd, w_br_gla=w_br_gla, w_br_ml=w_br_ml,
                  w_out=w_out, w_ffn_gate=w_ffn_gate, w_ffn_up=w_ffn_up, w_ffn_down=w_ffn_down)
    depth = w_in.shape[0]
    packed = [_pack_layer(params, l) for l in range(depth)]
    bp, lp_ = x_prompt.shape[0], x_prompt.shape[1]
    bs, ls = x_sample.shape[0], x_sample.shape[1]

    mod = _ada(jnp.concatenate([c_prompt, c_sample], axis=0), w_ada.astype(BF16), b_ada)
    mod_p, mod_s = mod[:, :bp], mod[:, bp:]

    zeros = lambda *shape: jnp.zeros((depth, bp) + shape, F32)
    zero_states = (zeros(SSD_CONV - 1, SSD_CONV_DIM), zeros(SSD_HEADS, SSD_HEAD_DIM, SSD_STATE),
                   zeros(GLA_HEADS, GLA_DK, GLA_DV), zeros(ML_HEADS, ML_DV, ML_DK), zeros(ML_HEADS, ML_DK),
                   zeros(ML_HEADS))
    q_p = 128 if lp_ % 128 == 0 else lp_
    y_p, st_p = _trunk(x_prompt, mod_p, zero_states, packed, q_p)
    sample_states = (state_ssd_conv, state_ssd, state_gla, state_mlstm_c, state_mlstm_n, state_mlstm_m)
    q_s = 128 if ls % 128 == 0 else ls
    y_s, st_s = _trunk(x_sample, mod_s, sample_states, packed, q_s)
    return (y_p, y_s) + st_p + st_s
```

```python
import functools
import math

import numpy as np
import jax
import jax.numpy as jnp
from jax import lax
from jax.experimental import pallas as pl
from jax.experimental.pallas import tpu as pltpu

F32 = jnp.float32
BF16 = jnp.bfloat16

D_MODEL = 1024
EPS = 1e-6
SSD_HEADS = 16
SSD_HEAD_DIM = 64
SSD_INNER = 1024
SSD_GROUPS = 2
SSD_STATE = 128
SSD_CONV = 4
SSD_CONV_DIM = 1536
GLA_HEADS = 4
GLA_DK = 128
GLA_DV = 256
GLA_RANK = 16
GLA_TAU = 16.0
ML_HEADS = 4
ML_DK = 256
ML_DV = 256
ML_INNER = 1024
FFN_HIDDEN = 2816

_SPLITS = (1024, 1536, 16, 512, 512, 1024, 1024, 16, 1024, 1024, 1024, 4, 4, 1024, 3072)
_OFF = np.concatenate([[0], np.cumsum(_SPLITS)]).tolist()
(_O_Z, _O_XBC, _O_DT, _O_GQ, _O_GK, _O_GV, _O_GR, _O_GLR, _O_MQ, _O_MK, _O_MV, _O_MI, _O_MF, _O_MO,
 _O_BR, _O_END) = _OFF

C_Z, C_XBC, C_GQ, C_GK, C_GV, C_GR, C_MQ, C_MK, C_MV, C_MO, N_MAIN = (
    0, 1024, 2560, 3072, 3584, 4608, 5632, 6656, 7680, 8704, 9728)
S_DT, S_GLR, S_MI, S_MF = 0, 16, 32, 36
N_SMALL = 128

VMEM_LIMIT = 56 * 1024 * 1024


def _cparams(sem):
    return pltpu.CompilerParams(dimension_semantics=sem, vmem_limit_bytes=VMEM_LIMIT)


def _sigmoid(x):
    return 1.0 / (1.0 + jnp.exp(-x))


def _silu(x):
    return x * _sigmoid(x)


def _softplus_tail(x):
    return jnp.log(1.0 + jnp.exp(-jnp.abs(x)))


def _nn(a, b):
    return jnp.dot(a, b, preferred_element_type=F32)


def _nt(a, b):
    return lax.dot_general(a, b, (((1,), (1,)), ((), ())), preferred_element_type=F32)


def _tn(a, b):
    return lax.dot_general(a, b, (((0,), (0,)), ((), ())), preferred_element_type=F32)


def _split(x, terms):
    parts = []
    for i in range(terms):
        p = x.astype(BF16)
        parts.append(p)
        if i + 1 < terms:
            x = x - p.astype(F32)
    return parts


def _sel_left(w01, x, fn=_nn, terms=3):
    parts = _split(x, terms)
    out = fn(w01, parts[0])
    for p in parts[1:]:
        out = out + fn(w01, p)
    return out


def _sel_right(x, w01, terms=3):
    parts = _split(x, terms)
    out = _nn(parts[0], w01)
    for p in parts[1:]:
        out = out + _nn(p, w01)
    return out


def _rms(x):
    return x * lax.rsqrt(jnp.mean(x * x, axis=-1, keepdims=True) + EPS)


def _group_rms(y, groups):
    n = y.shape[-1] // groups
    return jnp.concatenate([_rms(y[:, i * n:(i + 1) * n]) for i in range(groups)], axis=-1)


def _ada_kernel(c_ref, w_ref, b_ref, o_ref):
    c = c_ref[...]
    o_ref[0] = _nn(_silu(c).astype(BF16), w_ref[0]) + b_ref[0]


def _ada(c_all, w_ada, b_ada):
    depth = w_ada.shape[0]
    n = c_all.shape[0]
    d = D_MODEL
    return pl.pallas_call(
        _ada_kernel,
        grid=(depth, 6),
        in_specs=[pl.BlockSpec((n, d), lambda l, j: (0, 0)),
                  pl.BlockSpec((1, d, d), lambda l, j: (l, 0, j)),
                  pl.BlockSpec((1, 1, d), lambda l, j: (l, 0, j))],
        out_specs=pl.BlockSpec((1, n, d), lambda l, j: (l, 0, j)),
        out_shape=jax.ShapeDtypeStruct((depth, n, 6 * d), F32),
        compiler_params=_cparams(("parallel", "parallel")),
        name="ada",
    )(c_all, w_ada, b_ada.reshape(depth, 1, 6 * d))


def _prenorm(x, g, sh, sc):
    return _rms(x) * g * (1.0 + sc) + sh


def _gla_level_constants(q):
    nlev = int(math.log2(q))
    assert 1 << nlev == q
    wall = np.zeros((nlev + 1, q, q), np.float32)
    masks = np.zeros((nlev + 1, q, q), np.float32)
    wall[0] = np.tril(np.ones((q, q), np.float32))
    for lev in range(nlev):
        seg = q >> lev
        half = seg // 2
        for t in range(q):
            pos = t % seg
            start = t - pos
            bnd = start + half - 1
            if pos >= half:
                wall[lev + 1, t, bnd + 1:t + 1] = 1.0
                masks[lev, t, start:start + half] = 1.0
            else:
                wall[lev + 1, t, t + 1:bnd + 1] = 1.0
    masks[nlev] = np.eye(q, dtype=np.float32)
    return nlev, wall.reshape((nlev + 1) * q, q), masks


CONV_ROWS = 16
CARRY_ROWS = 128


def _conv_shift_constants(q):
    assert q >= CONV_ROWS
    taps = SSD_CONV - 1
    shift = np.zeros((taps * q, CARRY_ROWS + q), np.float32)
    for j in range(taps):
        d = taps - j
        for t in range(q):
            shift[j * q + t, CARRY_ROWS + t - d] = 1.0
            if t - d < 0:
                for term in (1, 2):
                    shift[j * q + t, CARRY_ROWS - term * CONV_ROWS + t - d] = 1.0
    return shift


PROJ_COLS = 512
GLA_SHORT_SUM = 8


def _projection_pieces(x_ref, mod_ref, gpre_ref, wmain_ref, wsmall_ref, main_ref, sm_ref, hb_ref):
    def head():
        hb = _prenorm(x_ref[...], gpre_ref[...], mod_ref[:, 0:1, :], mod_ref[:, 1:2, :])[0].astype(BF16)
        hb_ref[...] = hb
        sm_ref[...] = _nn(hb, wsmall_ref[...])

    def cols(j):
        cs = slice(j * PROJ_COLS, (j + 1) * PROJ_COLS)
        main_ref[:, cs] = _nn(hb_ref[...], wmain_ref[:, cs]).astype(BF16)

    return [head] + [functools.partial(cols, j) for j in range(N_MAIN // PROJ_COLS)]


class _Filler:
    def __init__(self, pieces):
        self._pieces = list(pieces)

    def __call__(self, count=1):
        for _ in range(count):
            if self._pieces:
                self._pieces.pop(0)()

    def drain(self):
        self(len(self._pieces))


def _mixer_kernel(x_ref, mod_ref, xn_ref, modn_ref, gpre_ref, wmain_ref, wsmall_ref,
                  conv0_ref, ssd0_ref, gla0_ref, mc0_ref, mn0_ref, mm0_ref,
                  convw_ref, convb_ref, rowp_ref, dx_ref, ssdnorm_ref, wgate_ref, bgate_ref, glanorm_ref,
                  mlbo_ref, mlnorm_ref, wall_ref, masks_ref, shift_ref,
                  y_ref, conv_ref, ssd_ref, gla_ref, mc_ref, mn_ref, mm_ref,
                  ext_ref, main0_ref, main1_ref, sm0_ref, sm1_ref, hb_ref, *, q, nlev):
    n = pl.program_id(0) * pl.num_programs(1) + pl.program_id(1)
    proj_w = (gpre_ref, wmain_ref, wsmall_ref)
    consts = (convw_ref, convb_ref, rowp_ref, dx_ref, ssdnorm_ref, wgate_ref, bgate_ref, glanorm_ref,
              mlbo_ref, mlnorm_ref, wall_ref, masks_ref, shift_ref)
    outs = (y_ref, conv_ref, ssd_ref, gla_ref, mc_ref, mn_ref, mm_ref, ext_ref)

    @pl.when(n == 0)
    def _first():
        _Filler(_projection_pieces(x_ref, mod_ref, *proj_w, main0_ref, sm0_ref, hb_ref)).drain()

    @pl.when(pl.program_id(1) > 0)
    def _carry():
        ext_ref[CARRY_ROWS - CONV_ROWS:CARRY_ROWS, :] = ext_ref[CARRY_ROWS + q - CONV_ROWS:CARRY_ROWS + q, :]
        ext_ref[CARRY_ROWS - 3 * CONV_ROWS:CARRY_ROWS - CONV_ROWS, :] = jnp.zeros(
            (2 * CONV_ROWS, SSD_CONV_DIM), BF16)

    @pl.when(pl.program_id(1) == 0)
    def _init():
        ext_ref[0:CARRY_ROWS, :] = jnp.zeros((CARRY_ROWS, SSD_CONV_DIM), BF16)
        for term, part in enumerate(_split(conv0_ref[0], 3)):
            ext_ref[CARRY_ROWS - (term + 1) * CONV_ROWS:CARRY_ROWS - term * CONV_ROWS, :] = part
        ssd_ref[...] = ssd0_ref[...]
        gla_ref[...] = gla0_ref[...]
        mc_ref[...] = mc0_ref[...]
        mn_ref[...] = mn0_ref[...]
        mm_ref[...] = mm0_ref[...]

    @pl.when(n % 2 == 0)
    def _even():
        fill = _Filler(_projection_pieces(xn_ref, modn_ref, *proj_w, main1_ref, sm1_ref, hb_ref))
        _mix_chunk(main0_ref, sm0_ref, *consts, *outs, fill, q=q, nlev=nlev)
        fill.drain()

    @pl.when(n % 2 == 1)
    def _odd():
        fill = _Filler(_projection_pieces(xn_ref, modn_ref, *proj_w, main0_ref, sm0_ref, hb_ref))
        _mix_chunk(main1_ref, sm1_ref, *consts, *outs, fill, q=q, nlev=nlev)
        fill.drain()


def _mix_chunk(main_ref, sm_ref,
               convw_ref, convb_ref, rowp_ref, dx_ref, ssdnorm_ref, wgate_ref, bgate_ref, glanorm_ref,
               mlbo_ref, mlnorm_ref, wall_ref, masks_ref, shift_ref,
               y_ref, conv_ref, ssd_ref, gla_ref, mc_ref, mn_ref, mm_ref, ext_ref, fill, *, q, nlev):
    sm = sm_ref[...]
    fill()
    row = lax.broadcasted_iota(jnp.int32, (q, q), 0)
    col = lax.broadcasted_iota(jnp.int32, (q, q), 1)
    causal = row >= col
    lane = lax.broadcasted_iota(jnp.int32, (1, 128), 1)
    lane_lo = lane < 64
    neg_inf = -jnp.inf

    smb = sm + rowp_ref[0:1, :]
    tail = _softplus_tail(smb)
    sp = jnp.maximum(smb, 0.0) + tail
    lsg = jnp.minimum(smb, 0.0) - tail
    a_row = -jnp.exp(rowp_ref[1:2, :])
    pre = jnp.where(lane < S_GLR, sp * a_row,
                    jnp.where(lane >= S_MF, jnp.where(lane < S_MF + ML_HEADS, lsg, 0.0),
                              jnp.where(lane >= S_MI, smb, 0.0)))
    tril = wall_ref[0:q, :]
    cums = _sel_left(tril, pre)
    eye = (lax.broadcasted_iota(jnp.int32, (128, 128), 0)
           == lax.broadcasted_iota(jnp.int32, (128, 128), 1)).astype(BF16)
    tr = _sel_left(eye, jnp.concatenate([cums, pre], axis=0), _nt)
    fill()

    xraw = main_ref[:, C_XBC:C_XBC + SSD_CONV_DIM]
    ext_ref[CARRY_ROWS:CARRY_ROWS + q, :] = xraw
    ysh = _nn(shift_ref[...], ext_ref[...])
    acc = convb_ref[...] + convw_ref[SSD_CONV - 1:SSD_CONV, :] * xraw.astype(F32)
    for j in range(SSD_CONV - 1):
        acc = acc + convw_ref[j:j + 1, :] * ysh[j * q:(j + 1) * q]
    conv_ref[0] = main_ref[q - CONV_ROWS:q, C_XBC:C_XBC + SSD_CONV_DIM].astype(F32)
    xc = _silu(acc)
    fill()
    xs = xc[:, :SSD_INNER]
    bm = xc[:, SSD_INNER:SSD_INNER + SSD_GROUPS * SSD_STATE]
    cm = xc[:, SSD_INNER + SSD_GROUPS * SSD_STATE:]

    expand = ((lax.broadcasted_iota(jnp.int32, (128, SSD_INNER), 1) >> 6)
              == lax.broadcasted_iota(jnp.int32, (128, SSD_INNER), 0)).astype(BF16)
    is_dt = lane < S_GLR
    acum = jnp.where(is_dt, cums, 0.0)
    dt = jnp.where(is_dt, sp, 0.0)
    eac = jnp.exp(acum)
    dtt = dt * jnp.exp(acum[q - 1:q, :] - acum)
    dx2 = _sel_right(jnp.concatenate([dt, dtt], axis=0), expand, terms=1)
    eac_x = _sel_right(eac, expand, terms=2)
    xdt = xs * dx2[0:q]
    xw = (xs * dx2[q:2 * q]).astype(BF16)
    y_parts = []
    hg = SSD_INNER // SSD_GROUPS
    for g in range(SSD_GROUPS):
        gs = slice(g * hg, (g + 1) * hg)
        cg = cm[:, g * SSD_STATE:(g + 1) * SSD_STATE].astype(BF16)
        bg = bm[:, g * SSD_STATE:(g + 1) * SSD_STATE].astype(BF16)
        wg = _nt(cg, bg)
        s_old = ssd_ref[0, g]
        y_inter = eac_x[:, gs] * _nn(cg, s_old.astype(BF16))
        ssd_ref[0, g] = s_old * eac_x[q - 1:q, gs] + _tn(bg, xw[:, gs])
        for pl_ in range(4):
            p = g * 4 + pl_
            xp = xdt[:, 128 * p:128 * p + 128]
            a2, x2 = [], []
            for j in range(2):
                h = 2 * p + j
                dec = jnp.exp(jnp.where(causal, cums[:, h:h + 1] - tr[h:h + 1, 0:q], neg_inf))
                a2.append((wg * dec).astype(BF16))
                x2.append(jnp.where(lane_lo if j == 0 else jnp.logical_not(lane_lo), xp, 0.0).astype(BF16))
            yp = _nn(jnp.concatenate(a2, axis=1), jnp.concatenate(x2, axis=0))
            y_parts.append(yp + y_inter[:, 128 * pl_:128 * pl_ + 128])
            fill()
    y = jnp.concatenate(y_parts, axis=-1) + xs * dx_ref[...]
    y = y * _silu(main_ref[:, C_Z:C_Z + SSD_INNER].astype(F32))
    y_ref[0, :, 0:SSD_INNER] = (_group_rms(y, SSD_GROUPS) * ssdnorm_ref[...]).astype(BF16)

    la_pre = _nn(sm.astype(BF16), wgate_ref[...]) + bgate_ref[...]
    la = (jnp.minimum(la_pre, 0.0) - _softplus_tail(la_pre)) * (1.0 / GLA_TAU)
    n_long = 1 + sum(1 for lev in range(nlev) if (q >> lev) // 2 > GLA_SHORT_SUM)
    la_hi, la_lo = _split(la, 2)
    e_long = _nn(wall_ref[0:n_long * q, :], la_hi) + _nn(wall_ref[0:n_long * q, :], la_lo)
    eall = [e_long[i * q:(i + 1) * q] for i in range(n_long)]
    if n_long < nlev + 1:
        e_short = _nn(wall_ref[n_long * q:(nlev + 1) * q, :], la_hi)
        eall += [e_short[i * q:(i + 1) * q] for i in range(nlev + 1 - n_long)]
    zeros_k = jnp.zeros((q, GLA_DK), BF16)
    o_parts = []
    for hp in range(GLA_HEADS // 2):
        qs, ks_, vs = [], [], []
        for j in range(2):
            h = 2 * hp + j
            qs.append(main_ref[:, C_GQ + GLA_DK * h:C_GQ + GLA_DK * (h + 1)].astype(F32) * (GLA_DK ** -0.5))
            ks_.append(main_ref[:, C_GK + GLA_DK * h:C_GK + GLA_DK * (h + 1)].astype(F32))
            vs.append(main_ref[:, C_GV + GLA_DV * h:C_GV + GLA_DV * (h + 1)])
        pair = slice(2 * GLA_DK * hp, 2 * GLA_DK * (hp + 1))

        def scores(scale):
            sc = [None, None] if scale is None else [scale[:, 0:GLA_DK], scale[:, GLA_DK:2 * GLA_DK]]
            qq = [(qs[j] if sc[j] is None else qs[j] * sc[j]).astype(BF16) for j in range(2)]
            kk = [(ks_[j] if sc[j] is None else ks_[j] * sc[j]).astype(BF16) for j in range(2)]
            kblk = jnp.concatenate([jnp.concatenate([kk[0], zeros_k], axis=1),
                                    jnp.concatenate([zeros_k, kk[1]], axis=1)], axis=0)
            return _nt(jnp.concatenate(qq, axis=1), kblk)

        att2 = scores(None) * masks_ref[nlev]
        for lev in range(nlev):
            att2 = att2 + scores(jnp.exp(eall[lev + 1][:, pair])) * masks_ref[lev]
            if lev % 2 == 1:
                fill()
        for j in range(2):
            h = 2 * hp + j
            bc = eall[0][:, GLA_DK * h:GLA_DK * (h + 1)]
            blast = bc[q - 1:q, :]
            st = gla_ref[0, h]
            o = (_nn(att2[:, j * q:(j + 1) * q].astype(BF16), vs[j])
                 + _nt((qs[j] * jnp.exp(bc)).astype(BF16), st.astype(BF16)))
            kt = (ks_[j] * jnp.exp(blast - bc)).astype(BF16)
            gla_ref[0, h] = st * jnp.exp(blast) + _tn(vs[j], kt)
            o_parts.append(_rms(o))
    gr = main_ref[:, C_GR:C_GR + GLA_HEADS * GLA_DV].astype(F32)
    y_ref[0, :, SSD_INNER:SSD_INNER + GLA_HEADS * GLA_DV] = (
        jnp.concatenate(o_parts, axis=-1) * glanorm_ref[...] * _silu(gr)).astype(BF16)

    mm_old = mm_ref[0]
    mm_new = jnp.zeros((1, 128), F32)
    h_parts = []
    for h in range(ML_HEADS):
        qh = main_ref[:, C_MQ + ML_DK * h:C_MQ + ML_DK * (h + 1)]
        kh = (main_ref[:, C_MK + ML_DK * h:C_MK + ML_DK * (h + 1)].astype(F32) * (ML_DK ** -0.5)).astype(BF16)
        vh = main_ref[:, C_MV + ML_DV * h:C_MV + ML_DV * (h + 1)]
        fcc = cums[:, S_MF + h:S_MF + h + 1]
        fcr = tr[S_MF + h:S_MF + h + 1, 0:q]
        igr = tr[S_MI + h:S_MI + h + 1, q:2 * q]
        igc = pre[:, S_MI + h:S_MI + h + 1]
        m0 = mm_old[:, h:h + 1]
        dmat = jnp.where(causal, fcc - fcr + igr, neg_inf)
        inter = fcc + m0
        mt = jnp.maximum(inter, jnp.max(dmat, axis=1, keepdims=True))
        w = _nt(qh, kh) * jnp.exp(dmat - mt)
        wi = jnp.exp(inter - mt)
        c_old = mc_ref[0, h]
        n_old = mn_ref[0, h:h + 1, :]
        num = _nn(w.astype(BF16), vh) + wi * _nt(qh, c_old.astype(BF16))
        den = jnp.sum(w, axis=1, keepdims=True) + wi * jnp.sum(qh.astype(F32) * n_old, axis=1, keepdims=True)
        hb = num / jnp.maximum(jnp.abs(den), jnp.exp(-mt))
        m_new = mt[q - 1:q, :]
        flast = fcc[q - 1:q, :]
        wt = jnp.exp(flast - fcc + igc - m_new)
        dc = jnp.exp(flast + m0 - m_new)
        mc_ref[0, h] = c_old * dc + _tn((vh.astype(F32) * wt).astype(BF16), kh)
        mn_ref[0, h:h + 1, :] = n_old * dc + jnp.sum(wt * kh.astype(F32), axis=0, keepdims=True)
        mm_new = jnp.where(lane == h, m_new, mm_new)
        fill()
        og = _sigmoid(main_ref[:, C_MO + ML_DV * h:C_MO + ML_DV * (h + 1)].astype(F32)
                      + mlbo_ref[:, ML_DV * h:ML_DV * (h + 1)])
        h_parts.append(_rms(og * hb))
    mm_ref[0] = mm_new
    y_ref[0, :, 2 * D_MODEL:3 * D_MODEL] = (jnp.concatenate(h_parts, axis=-1) * mlnorm_ref[...]).astype(BF16)


def _mixers(x, mod3, st, lp, q):
    b, l, d = x.shape
    conv0, ssd0, gla0t, mc0, mn0, mm0 = st
    resident = lambda a: pl.BlockSpec(a.shape, lambda i, c: (0,) * a.ndim, pipeline_mode=pl.Buffered(1))
    n_chunks = l // q

    def nxt(i, c):
        n = jnp.minimum(i * n_chunks + c + 1, b * n_chunks - 1)
        return n // n_chunks, n % n_chunks
    nlev, wall, masks = _gla_level_constants(q)
    wall = jnp.asarray(wall, BF16)
    masks = jnp.asarray(np.concatenate([masks, masks], axis=-1), F32)
    shift = _conv_shift_constants(q)
    state_dims = [(CONV_ROWS, SSD_CONV_DIM), (SSD_GROUPS, SSD_STATE, SSD_INNER // SSD_GROUPS),
                  (GLA_HEADS, GLA_DV, GLA_DK),
                  (ML_HEADS, ML_DV, ML_DK), (ML_HEADS, ML_DK), (1, 128)]
    per_b = lambda shape, **kw: pl.BlockSpec((1,) + shape, lambda i, c: (i,) + (0,) * len(shape), **kw)
    consts = [lp['conv_w'], lp['conv_b'], lp['rowp'], lp['d_x'], lp['ssd_norm'], lp['w_gate'], lp['b_gate'],
              lp['gla_norm'], lp['ml_b_o'], lp['ml_norm'], wall, masks,
              jnp.asarray(shift, BF16)]
    state_shapes = [jax.ShapeDtypeStruct(s.shape, F32) for s in (conv0, ssd0, gla0t, mc0, mn0, mm0)]
    return pl.pallas_call(
        functools.partial(_mixer_kernel, q=q, nlev=nlev),
        grid=(b, l // q),
        in_specs=[pl.BlockSpec((1, q, d), lambda i, c: (0, 0, 0), pipeline_mode=pl.Buffered(1)),
                  pl.BlockSpec((1, 6, d), lambda i, c: (0, 0, 0), pipeline_mode=pl.Buffered(1)),
                  pl.BlockSpec((1, q, d), lambda i, c: nxt(i, c) + (0,)),
                  pl.BlockSpec((1, 6, d), lambda i, c: (nxt(i, c)[0], 0, 0)),
                  resident(lp['g_pre_mix']), resident(lp['w_main']), resident(lp['w_small'])]
        + [per_b(s, pipeline_mode=pl.Buffered(1)) for s in state_dims] + [resident(a) for a in consts],
        out_specs=[pl.BlockSpec((1, q, 3 * D_MODEL), lambda i, c: (i, c, 0))] + [per_b(s) for s in state_dims],
        out_shape=[jax.ShapeDtypeStruct((b, l, 3 * D_MODEL), BF16)] + state_shapes,
        scratch_shapes=[pltpu.VMEM((CARRY_ROWS + q, SSD_CONV_DIM), BF16),
                        pltpu.VMEM((q, N_MAIN), BF16), pltpu.VMEM((q, N_MAIN), BF16),
                        pltpu.VMEM((q, N_SMALL), F32), pltpu.VMEM((q, N_SMALL), F32),
                        pltpu.VMEM((q, d), BF16)],
        compiler_params=_cparams(("arbitrary", "arbitrary")),
        name="mixers",
    )(x, mod3, x, mod3, lp['g_pre_mix'], lp['w_main'], lp['w_small'], conv0, ssd0, gla0t, mc0, mn0, mm0, *consts)


def _merge_kernel(x_ref, y_ref, mod_ref, gpre_ref, gpost_ref, wbr_ref, bbr_ref, wabc_ref, wout_ref, o_ref):
    tb, tl, d = x_ref.shape
    x = x_ref[...]
    h = _prenorm(x, gpre_ref[...], mod_ref[:, 0:1, :], mod_ref[:, 1:2, :])
    hb = h.reshape(tb * tl, d).astype(BF16)
    y = y_ref[...].reshape(tb * tl, 3 * d)
    merged = None
    for i in range(3):
        gate = _sigmoid(_nn(hb, wbr_ref[:, i * d:(i + 1) * d]) + bbr_ref[:, i * d:(i + 1) * d])
        t = gate * _nn(y[:, i * d:(i + 1) * d], wabc_ref[i])
        merged = t if merged is None else merged + t
    mix = _nn(merged.astype(BF16), wout_ref[...])
    o_ref[...] = x + mod_ref[:, 2:3, :] * (_rms(mix) * gpost_ref[...]).reshape(tb, tl, d)


def _merge(x, y, mod3, lp, tb, tl):
    b, l, d = x.shape
    const = lambda a: pl.BlockSpec(a.shape, lambda i, k: (0,) * a.ndim)
    consts = [lp['g_pre_mix'], lp['g_post_mix'], lp['w_br'], lp['b_br'], lp['w_abc'], lp['w_out']]
    return pl.pallas_call(
        _merge_kernel,
        grid=(b // tb, l // tl),
        in_specs=[pl.BlockSpec((tb, tl, d), lambda i, k: (i, k, 0)),
                  pl.BlockSpec((tb, tl, 3 * d), lambda i, k: (i, k, 0)),
                  pl.BlockSpec((tb, 6, d), lambda i, k: (i, 0, 0))] + [const(a) for a in consts],
        out_specs=pl.BlockSpec((tb, tl, d), lambda i, k: (i, k, 0)),
        out_shape=jax.ShapeDtypeStruct((b, l, d), F32),
        compiler_params=_cparams(("parallel", "parallel")),
        name="merge",
    )(x, y, mod3, *consts)


def _ffn_kernel(x_ref, mod_ref, gpre_ref, gpost_ref, wg_ref, wu_ref, wd_ref, o_ref, *, th):
    tb, tl, d = x_ref.shape
    x = x_ref[...]
    hb = _prenorm(x, gpre_ref[...], mod_ref[:, 3:4, :], mod_ref[:, 4:5, :]).reshape(tb * tl, d).astype(BF16)
    f = None
    for j in range(FFN_HIDDEN // th):
        hs = slice(j * th, (j + 1) * th)
        a = _nn(hb, wg_ref[:, hs])
        u = _nn(hb, wu_ref[:, hs])
        t = _nn((_silu(a) * u).astype(BF16), wd_ref[hs, :])
        f = t if f is None else f + t
    o_ref[...] = x + mod_ref[:, 5:6, :] * (_rms(f) * gpost_ref[...]).reshape(tb, tl, d)


def _ffn(x, mod3, lp, tb, tl, th):
    b, l, d = x.shape
    resident = lambda a: pl.BlockSpec(a.shape, lambda i, k: (0,) * a.ndim, pipeline_mode=pl.Buffered(1))
    return pl.pallas_call(
        functools.partial(_ffn_kernel, th=th),
        grid=(b // tb, l // tl),
        in_specs=[pl.BlockSpec((tb, tl, d), lambda i, k: (i, k, 0)),
                  pl.BlockSpec((tb, 6, d), lambda i, k: (i, 0, 0)),
                  resident(lp['g_pre_ffn']), resident(lp['g_post_ffn']),
                  resident(lp['w_ffn_gate']), resident(lp['w_ffn_up']), resident(lp['w_ffn_down'])],
        out_specs=pl.BlockSpec((tb, tl, d), lambda i, k: (i, k, 0)),
        out_shape=jax.ShapeDtypeStruct((b, l, d), F32),
        compiler_params=_cparams(("parallel", "parallel")),
        name="ffn",
    )(x, mod3, lp['g_pre_ffn'], lp['g_post_ffn'], lp['w_ffn_gate'], lp['w_ffn_up'], lp['w_ffn_down'])


def _pack_layer(p, l):
    d = D_MODEL
    w_in = p['w_in'][l]
    cut = lambda a, n: w_in[:, a:a + n]
    w_main = jnp.concatenate([cut(_O_Z, 1024), cut(_O_XBC, 1536), cut(_O_GQ, 512), cut(_O_GK, 512),
                              cut(_O_GV, 1024), cut(_O_GR, 1024), cut(_O_MQ, 1024), cut(_O_MK, 1024),
                              cut(_O_MV, 1024), cut(_O_MO, 1024)], axis=1).astype(BF16)
    w_small = jnp.concatenate([cut(_O_DT, 16), cut(_O_GLR, 16), cut(_O_MI, 4), cut(_O_MF, 4),
                               jnp.zeros((d, N_SMALL - 40), F32)], axis=1).astype(BF16)
    z = lambda n: jnp.zeros((n,), F32)
    bias_row = jnp.concatenate([p['ssd_dt_bias'][l], z(16), p['ml_b_i'][l], p['ml_b_f'][l], z(N_SMALL - 40)])
    alog_row = jnp.concatenate([p['ssd_a_log'][l], z(N_SMALL - 16)])
    rowp = jnp.zeros((8, N_SMALL), F32).at[0].set(bias_row).at[1].set(alog_row)
    w_gate = jnp.zeros((N_SMALL, GLA_HEADS * GLA_DK), F32).at[S_GLR:S_GLR + GLA_RANK].set(p['gla_w_gate'][l])
    row = lambda a: a.reshape(1, -1)
    return dict(
        w_main=w_main, w_small=w_small,
        w_br=cut(_O_BR, 3 * d).astype(BF16), b_br=row(p['b_branch'][l]),
        w_abc=jnp.stack([p['w_br_ssd'][l], p['w_br_gla'][l], p['w_br_ml'][l]]).astype(BF16),
        w_out=p['w_out'][l].astype(BF16),
        w_ffn_gate=p['w_ffn_gate'][l].astype(BF16), w_ffn_up=p['w_ffn_up'][l].astype(BF16),
        w_ffn_down=p['w_ffn_down'][l].astype(BF16),
        g_pre_mix=p['g_pre_mix'][l].reshape(1, 1, d), g_post_mix=row(p['g_post_mix'][l]),
        g_pre_ffn=p['g_pre_ffn'][l].reshape(1, 1, d), g_post_ffn=row(p['g_post_ffn'][l]),
        conv_w=p['ssd_conv_w'][l], conv_b=row(p['ssd_conv_b'][l]), rowp=rowp,
        d_x=row(jnp.repeat(p['ssd_d'][l], SSD_HEAD_DIM)), ssd_norm=row(p['ssd_norm'][l]),
        w_gate=w_gate.astype(BF16), b_gate=row(p['gla_b_gate'][l]), gla_norm=row(p['gla_norm'][l]),
        ml_b_o=row(p['ml_b_o'][l]), ml_norm=row(p['ml_norm'][l]))


MIX_CHUNK = 128


def _chunk_len(l):
    return MIX_CHUNK if l % MIX_CHUNK == 0 else l


MERGE_ROWS = 512
FFN_ROWS = 1024
FFN_HIDDEN_CHUNK = 704


def _tiles(b, l, rows):
    if l >= rows:
        return 1, rows
    return b, l


def _trunk(x, mod, states, packed, q):
    b, l, d = x.shape
    depth = len(packed)
    new = []
    for li in range(depth):
        lp = packed[li]
        mod3 = mod[li].reshape(b, 6, d)
        conv0, ssd0, gla0, mc0, mn0, mm0 = (s[li] for s in states)
        st = (jnp.pad(conv0, ((0, 0), (CONV_ROWS - (SSD_CONV - 1), 0), (0, 0))),
              jnp.swapaxes(ssd0.reshape(b, SSD_GROUPS, SSD_INNER // SSD_GROUPS, SSD_STATE), -1, -2),
              jnp.swapaxes(gla0, -1, -2), mc0, mn0,
              jnp.pad(mm0, ((0, 0), (0, 128 - ML_HEADS))).reshape(b, 1, 128))
        y, conv_n, ssd_n, gla_n, mc_n, mn_n, mm_n = _mixers(x, mod3, st, lp, q)
        x = _merge(x, y, mod3, lp, *_tiles(b, l, MERGE_ROWS))
        x = _ffn(x, mod3, lp, *_tiles(b, l, FFN_ROWS), FFN_HIDDEN_CHUNK)
        new.append((conv_n[:, CONV_ROWS - (SSD_CONV - 1):], jnp.swapaxes(ssd_n, -1, -2).reshape(b, SSD_HEADS, SSD_HEAD_DIM, SSD_STATE),
                    jnp.swapaxes(gla_n, -1, -2),
                    mc_n, mn_n, mm_n[:, 0, :ML_HEADS]))
    return x, tuple(jnp.stack([st[i] for st in new]) for i in range(6))


def kernel(x_prompt, x_sample, c_prompt, c_sample, state_ssd_conv, state_ssd, state_gla, state_mlstm_c, state_mlstm_n, state_mlstm_m, w_ada, b_ada, g_pre_mix, g_post_mix, g_pre_ffn, g_post_ffn, w_in, ssd_conv_w, ssd_conv_b, ssd_dt_bias, ssd_a_log, ssd_d, ssd_norm, gla_w_gate, gla_b_gate, gla_norm, ml_b_i, ml_b_f, ml_b_o, ml_norm, b_branch, w_br_ssd, w_br_gla, w_br_ml, w_out, w_ffn_gate, w_ffn_up, w_ffn_down):
    params = dict(g_pre_mix=g_pre_mix, g_post_mix=g_post_mix, g_pre_ffn=g_pre_ffn, g_post_ffn=g_post_ffn,
                  w_in=w_in, ssd_conv_w=ssd_conv_w, ssd_conv_b=ssd_conv_b, ssd_dt_bias=ssd_dt_bias,
                  ssd_a_log=ssd_a_log, ssd_d=ssd_d, ssd_norm=ssd_norm, gla_w_gate=gla_w_gate,
                  gla_b_gate=gla_b_gate, gla_norm=gla_norm, ml_b_i=ml_b_i, ml_b_f=ml_b_f, ml_b_o=ml_b_o,
                  ml_norm=ml_norm, b_branch=b_branch, w_br_ssd=w_br_ssd, w_br_gla=w_br_gla, w_br_ml=w_br_ml,
                  w_out=w_out, w_ffn_gate=w_ffn_gate, w_ffn_up=w_ffn_up, w_ffn_down=w_ffn_down)
    depth = w_in.shape[0]
    packed = [_pack_layer(params, l) for l in range(depth)]
    bp, lp_ = x_prompt.shape[0], x_prompt.shape[1]
    bs, ls = x_sample.shape[0], x_sample.shape[1]

    mod = _ada(jnp.concatenate([c_prompt, c_sample], axis=0), w_ada.astype(BF16), b_ada)
    mod_p, mod_s = mod[:, :bp], mod[:, bp:]

    zeros = lambda *shape: jnp.zeros((depth, bp) + shape, F32)
    zero_states = (zeros(SSD_CONV - 1, SSD_CONV_DIM), zeros(SSD_HEADS, SSD_HEAD_DIM, SSD_STATE),
                   zeros(GLA_HEADS, GLA_DK, GLA_DV), zeros(ML_HEADS, ML_DV, ML_DK), zeros(ML_HEADS, ML_DK),
                   zeros(ML_HEADS))
    y_p, st_p = _trunk(x_prompt, mod_p, zero_states, packed, _chunk_len(lp_))
    sample_states = (state_ssd_conv, state_ssd, state_gla, state_mlstm_c, state_mlstm_n, state_mlstm_m)
    y_s, st_s = _trunk(x_sample, mod_s, sample_states, packed, _chunk_len(ls))
    return (y_p, y_s) + st_p + st_s
```

```python
import functools
import math

import numpy as np
import jax
import jax.numpy as jnp
from jax import lax
from jax.experimental import pallas as pl
from jax.experimental.pallas import tpu as pltpu

F32 = jnp.float32
BF16 = jnp.bfloat16

D_MODEL = 1024
EPS = 1e-6
SSD_HEADS = 16
SSD_HEAD_DIM = 64
SSD_INNER = 1024
SSD_GROUPS = 2
SSD_STATE = 128
SSD_CONV = 4
SSD_CONV_DIM = 1536
GLA_HEADS = 4
GLA_DK = 128
GLA_DV = 256
GLA_RANK = 16
GLA_TAU = 16.0
ML_HEADS = 4
ML_DK = 256
ML_DV = 256
ML_INNER = 1024
FFN_HIDDEN = 2816

_SPLITS = (1024, 1536, 16, 512, 512, 1024, 1024, 16, 1024, 1024, 1024, 4, 4, 1024, 3072)
_OFF = np.concatenate([[0], np.cumsum(_SPLITS)]).tolist()
(_O_Z, _O_XBC, _O_DT, _O_GQ, _O_GK, _O_GV, _O_GR, _O_GLR, _O_MQ, _O_MK, _O_MV, _O_MI, _O_MF, _O_MO,
 _O_BR, _O_END) = _OFF

C_Z, C_XBC, C_GQ, C_GK, C_GV, C_GR, C_MQ, C_MK, C_MV, C_MO, N_MAIN = (
    0, 1024, 2560, 3072, 3584, 4608, 5632, 6656, 7680, 8704, 9728)
S_DT, S_GLR, S_MI, S_MF = 0, 16, 32, 36
N_SMALL = 128

VMEM_LIMIT = 56 * 1024 * 1024


def _cparams(sem):
    return pltpu.CompilerParams(dimension_semantics=sem, vmem_limit_bytes=VMEM_LIMIT)


def _sigmoid(x):
    return 1.0 / (1.0 + jnp.exp(-x))


def _silu(x):
    return x * _sigmoid(x)


def _softplus_tail(x):
    return jnp.log(1.0 + jnp.exp(-jnp.abs(x)))


def _nn(a, b):
    return jnp.dot(a, b, preferred_element_type=F32)


def _nt(a, b):
    return lax.dot_general(a, b, (((1,), (1,)), ((), ())), preferred_element_type=F32)


def _tn(a, b):
    return lax.dot_general(a, b, (((0,), (0,)), ((), ())), preferred_element_type=F32)


def _split(x, terms):
    parts = []
    for i in range(terms):
        p = x.astype(BF16)
        parts.append(p)
        if i + 1 < terms:
            x = x - p.astype(F32)
    return parts


def _sel_left(w01, x, fn=_nn, terms=3):
    parts = _split(x, terms)
    out = fn(w01, parts[0])
    for p in parts[1:]:
        out = out + fn(w01, p)
    return out


def _sel_right(x, w01, terms=3):
    parts = _split(x, terms)
    out = _nn(parts[0], w01)
    for p in parts[1:]:
        out = out + _nn(p, w01)
    return out


def _rms(x):
    return x * lax.rsqrt(jnp.mean(x * x, axis=-1, keepdims=True) + EPS)


def _group_rms(y, groups):
    n = y.shape[-1] // groups
    return jnp.concatenate([_rms(y[:, i * n:(i + 1) * n]) for i in range(groups)], axis=-1)


def _ada_kernel(c_ref, w_ref, b_ref, o_ref):
    c = c_ref[...]
    o_ref[0] = _nn(_silu(c).astype(BF16), w_ref[0]) + b_ref[0]


def _ada(c_all, w_ada, b_ada):
    depth = w_ada.shape[0]
    n = c_all.shape[0]
    d = D_MODEL
    return pl.pallas_call(
        _ada_kernel,
        grid=(depth, 6),
        in_specs=[pl.BlockSpec((n, d), lambda l, j: (0, 0)),
                  pl.BlockSpec((1, d, d), lambda l, j: (l, 0, j)),
                  pl.BlockSpec((1, 1, d), lambda l, j: (l, 0, j))],
        out_specs=pl.BlockSpec((1, n, d), lambda l, j: (l, 0, j)),
        out_shape=jax.ShapeDtypeStruct((depth, n, 6 * d), F32),
        compiler_params=_cparams(("parallel", "parallel")),
        name="ada",
    )(c_all, w_ada, b_ada.reshape(depth, 1, 6 * d))


def _prenorm(x, g, sh, sc):
    return _rms(x) * g * (1.0 + sc) + sh


def _gla_level_constants(q):
    nlev = int(math.log2(q))
    assert 1 << nlev == q
    wall = np.zeros((nlev + 1, q, q), np.float32)
    masks = np.zeros((nlev + 1, q, q), np.float32)
    wall[0] = np.tril(np.ones((q, q), np.float32))
    for lev in range(nlev):
        seg = q >> lev
        half = seg // 2
        for t in range(q):
            pos = t % seg
            start = t - pos
            bnd = start + half - 1
            if pos >= half:
                wall[lev + 1, t, bnd + 1:t + 1] = 1.0
                masks[lev, t, start:start + half] = 1.0
            else:
                wall[lev + 1, t, t + 1:bnd + 1] = 1.0
    masks[nlev] = np.eye(q, dtype=np.float32)
    return nlev, wall.reshape((nlev + 1) * q, q), masks


CONV_ROWS = 16
CARRY_ROWS = 128


def _conv_shift_constants(q):
    assert q >= CONV_ROWS
    taps = SSD_CONV - 1
    shift = np.zeros((taps * q, CARRY_ROWS + q), np.float32)
    for j in range(taps):
        d = taps - j
        for t in range(q):
            shift[j * q + t, CARRY_ROWS + t - d] = 1.0
            if t - d < 0:
                for term in (1, 2):
                    shift[j * q + t, CARRY_ROWS - term * CONV_ROWS + t - d] = 1.0
    return shift


PROJ_COLS = 512
GLA_SHORT_SUM = 8
SUB_ROWS = 256


def _projection_pieces(x_ref, mod_ref, gpre_ref, wmain_ref, wsmall_ref, main_ref, sm_ref, hb_ref):
    def head():
        hb = _prenorm(x_ref[...], gpre_ref[...], mod_ref[:, 0:1, :], mod_ref[:, 1:2, :])[0].astype(BF16)
        hb_ref[...] = hb
        sm_ref[...] = _nn(hb, wsmall_ref[...])

    def cols(j):
        cs = slice(j * PROJ_COLS, (j + 1) * PROJ_COLS)
        main_ref[:, cs] = _nn(hb_ref[...], wmain_ref[:, cs]).astype(BF16)

    return [head] + [functools.partial(cols, j) for j in range(N_MAIN // PROJ_COLS)]


class _Filler:
    def __init__(self, pieces):
        self._pieces = list(pieces)

    def __call__(self, count=1):
        for _ in range(count):
            if self._pieces:
                self._pieces.pop(0)()

    def drain(self):
        self(len(self._pieces))


def _mixer_kernel(x_ref, mod_ref, xn_ref, modn_ref, gpre_ref, wmain_ref, wsmall_ref,
                  conv0_ref, ssd0_ref, gla0_ref, mc0_ref, mn0_ref, mm0_ref,
                  convw_ref, convb_ref, rowp_ref, dx_ref, ssdnorm_ref, wgate_ref, bgate_ref, glanorm_ref,
                  mlbo_ref, mlnorm_ref, wall_ref, masks_ref, shift_ref,
                  y_ref, conv_ref, ssd_ref, gla_ref, mc_ref, mn_ref, mm_ref,
                  ext_ref, main0_ref, main1_ref, sm0_ref, sm1_ref, hb_ref, *, q, nlev):
    n = pl.program_id(0) * pl.num_programs(1) + pl.program_id(1)
    proj_w = (gpre_ref, wmain_ref, wsmall_ref)
    consts = (convw_ref, convb_ref, rowp_ref, dx_ref, ssdnorm_ref, wgate_ref, bgate_ref, glanorm_ref,
              mlbo_ref, mlnorm_ref, wall_ref, masks_ref, shift_ref)
    outs = (y_ref, conv_ref, ssd_ref, gla_ref, mc_ref, mn_ref, mm_ref, ext_ref)

    @pl.when(n == 0)
    def _first():
        _Filler(_projection_pieces(x_ref, mod_ref, *proj_w, main0_ref, sm0_ref, hb_ref)).drain()

    @pl.when(pl.program_id(1) > 0)
    def _carry():
        ext_ref[CARRY_ROWS - CONV_ROWS:CARRY_ROWS, :] = ext_ref[CARRY_ROWS + q - CONV_ROWS:CARRY_ROWS + q, :]
        ext_ref[CARRY_ROWS - 3 * CONV_ROWS:CARRY_ROWS - CONV_ROWS, :] = jnp.zeros(
            (2 * CONV_ROWS, SSD_CONV_DIM), BF16)

    @pl.when(pl.program_id(1) == 0)
    def _init():
        ext_ref[0:CARRY_ROWS, :] = jnp.zeros((CARRY_ROWS, SSD_CONV_DIM), BF16)
        for term, part in enumerate(_split(conv0_ref[0], 3)):
            ext_ref[CARRY_ROWS - (term + 1) * CONV_ROWS:CARRY_ROWS - term * CONV_ROWS, :] = part
        ssd_ref[...] = ssd0_ref[...]
        gla_ref[...] = gla0_ref[...]
        mc_ref[...] = mc0_ref[...]
        mn_ref[...] = mn0_ref[...]
        mm_ref[...] = mm0_ref[...]

    @pl.when(n % 2 == 0)
    def _even():
        fill = _Filler(_projection_pieces(xn_ref, modn_ref, *proj_w, main1_ref, sm1_ref, hb_ref))
        _mix_chunk(main0_ref, sm0_ref, *consts, *outs, fill, q=q, nlev=nlev)
        fill.drain()

    @pl.when(n % 2 == 1)
    def _odd():
        fill = _Filler(_projection_pieces(xn_ref, modn_ref, *proj_w, main0_ref, sm0_ref, hb_ref))
        _mix_chunk(main1_ref, sm1_ref, *consts, *outs, fill, q=q, nlev=nlev)
        fill.drain()


def _mix_chunk(main_ref, sm_ref,
               convw_ref, convb_ref, rowp_ref, dx_ref, ssdnorm_ref, wgate_ref, bgate_ref, glanorm_ref,
               mlbo_ref, mlnorm_ref, wall_ref, masks_ref, shift_ref,
               y_ref, conv_ref, ssd_ref, gla_ref, mc_ref, mn_ref, mm_ref, ext_ref, fill, *, q, nlev):
    sm = sm_ref[...]
    fill()
    row = lax.broadcasted_iota(jnp.int32, (q, q), 0)
    col = lax.broadcasted_iota(jnp.int32, (q, q), 1)
    causal = row >= col
    lane = lax.broadcasted_iota(jnp.int32, (1, 128), 1)
    lane_lo = lane < 64
    neg_inf = -jnp.inf

    smb = sm + rowp_ref[0:1, :]
    tail = _softplus_tail(smb)
    sp = jnp.maximum(smb, 0.0) + tail
    lsg = jnp.minimum(smb, 0.0) - tail
    a_row = -jnp.exp(rowp_ref[1:2, :])
    pre = jnp.where(lane < S_GLR, sp * a_row,
                    jnp.where(lane >= S_MF, jnp.where(lane < S_MF + ML_HEADS, lsg, 0.0),
                              jnp.where(lane >= S_MI, smb, 0.0)))
    tril = wall_ref[0:q, :]
    cums = _sel_left(tril, pre)
    eye = (lax.broadcasted_iota(jnp.int32, (128, 128), 0)
           == lax.broadcasted_iota(jnp.int32, (128, 128), 1)).astype(BF16)
    tr = _sel_left(eye, jnp.concatenate([cums, pre], axis=0), _nt)
    fill()

    xraw = main_ref[:, C_XBC:C_XBC + SSD_CONV_DIM]
    ext_ref[CARRY_ROWS:CARRY_ROWS + q, :] = xraw
    ysh = _nn(shift_ref[...], ext_ref[...])
    acc = convb_ref[...] + convw_ref[SSD_CONV - 1:SSD_CONV, :] * xraw.astype(F32)
    for j in range(SSD_CONV - 1):
        acc = acc + convw_ref[j:j + 1, :] * ysh[j * q:(j + 1) * q]
    conv_ref[0] = main_ref[q - CONV_ROWS:q, C_XBC:C_XBC + SSD_CONV_DIM].astype(F32)
    xc = _silu(acc)
    fill()
    xs = xc[:, :SSD_INNER]
    bm = xc[:, SSD_INNER:SSD_INNER + SSD_GROUPS * SSD_STATE]
    cm = xc[:, SSD_INNER + SSD_GROUPS * SSD_STATE:]

    expand = ((lax.broadcasted_iota(jnp.int32, (128, SSD_INNER), 1) >> 6)
              == lax.broadcasted_iota(jnp.int32, (128, SSD_INNER), 0)).astype(BF16)
    is_dt = lane < S_GLR
    acum = jnp.where(is_dt, cums, 0.0)
    dt = jnp.where(is_dt, sp, 0.0)
    eac = jnp.exp(acum)
    dtt = dt * jnp.exp(acum[q - 1:q, :] - acum)
    dx2 = _sel_right(jnp.concatenate([dt, dtt], axis=0), expand, terms=1)
    eac_x = _sel_right(eac, expand, terms=2)
    xdt = xs * dx2[0:q]
    xw = (xs * dx2[q:2 * q]).astype(BF16)
    y_parts = []
    hg = SSD_INNER // SSD_GROUPS
    for g in range(SSD_GROUPS):
        gs = slice(g * hg, (g + 1) * hg)
        cg = cm[:, g * SSD_STATE:(g + 1) * SSD_STATE].astype(BF16)
        bg = bm[:, g * SSD_STATE:(g + 1) * SSD_STATE].astype(BF16)
        wg = _nt(cg, bg)
        s_old = ssd_ref[0, g]
        y_inter = eac_x[:, gs] * _nn(cg, s_old.astype(BF16))
        ssd_ref[0, g] = s_old * eac_x[q - 1:q, gs] + _tn(bg, xw[:, gs])
        for pl_ in range(4):
            p = g * 4 + pl_
            xp = xdt[:, 128 * p:128 * p + 128]
            a2, x2 = [], []
            for j in range(2):
                h = 2 * p + j
                dec = jnp.exp(jnp.where(causal, cums[:, h:h + 1] - tr[h:h + 1, 0:q], neg_inf))
                a2.append((wg * dec).astype(BF16))
                x2.append(jnp.where(lane_lo if j == 0 else jnp.logical_not(lane_lo), xp, 0.0).astype(BF16))
            yp = _nn(jnp.concatenate(a2, axis=1), jnp.concatenate(x2, axis=0))
            y_parts.append(yp + y_inter[:, 128 * pl_:128 * pl_ + 128])
            fill()
    y = jnp.concatenate(y_parts, axis=-1) + xs * dx_ref[...]
    y = y * _silu(main_ref[:, C_Z:C_Z + SSD_INNER].astype(F32))
    y_ref[0, :, 0:SSD_INNER] = (_group_rms(y, SSD_GROUPS) * ssdnorm_ref[...]).astype(BF16)

    la_pre = _nn(sm.astype(BF16), wgate_ref[...]) + bgate_ref[...]
    la = (jnp.minimum(la_pre, 0.0) - _softplus_tail(la_pre)) * (1.0 / GLA_TAU)
    n_long = 1 + sum(1 for lev in range(nlev) if (q >> lev) // 2 > GLA_SHORT_SUM)
    la_hi, la_lo = _split(la, 2)
    e_long = _nn(wall_ref[0:n_long * q, :], la_hi) + _nn(wall_ref[0:n_long * q, :], la_lo)
    eall = [e_long[i * q:(i + 1) * q] for i in range(n_long)]
    if n_long < nlev + 1:
        e_short = _nn(wall_ref[n_long * q:(nlev + 1) * q, :], la_hi)
        eall += [e_short[i * q:(i + 1) * q] for i in range(nlev + 1 - n_long)]
    zeros_k = jnp.zeros((q, GLA_DK), BF16)
    o_parts = []
    for hp in range(GLA_HEADS // 2):
        qs, ks_, vs = [], [], []
        for j in range(2):
            h = 2 * hp + j
            qs.append(main_ref[:, C_GQ + GLA_DK * h:C_GQ + GLA_DK * (h + 1)].astype(F32) * (GLA_DK ** -0.5))
            ks_.append(main_ref[:, C_GK + GLA_DK * h:C_GK + GLA_DK * (h + 1)].astype(F32))
            vs.append(main_ref[:, C_GV + GLA_DV * h:C_GV + GLA_DV * (h + 1)])
        pair = slice(2 * GLA_DK * hp, 2 * GLA_DK * (hp + 1))

        def scores(scale):
            sc = [None, None] if scale is None else [scale[:, 0:GLA_DK], scale[:, GLA_DK:2 * GLA_DK]]
            qq = [(qs[j] if sc[j] is None else qs[j] * sc[j]).astype(BF16) for j in range(2)]
            kk = [(ks_[j] if sc[j] is None else ks_[j] * sc[j]).astype(BF16) for j in range(2)]
            kblk = jnp.concatenate([jnp.concatenate([kk[0], zeros_k], axis=1),
                                    jnp.concatenate([zeros_k, kk[1]], axis=1)], axis=0)
            return _nt(jnp.concatenate(qq, axis=1), kblk)

        att2 = scores(None) * masks_ref[nlev]
        for lev in range(nlev):
            att2 = att2 + scores(jnp.exp(eall[lev + 1][:, pair])) * masks_ref[lev]
            if lev % 2 == 1:
                fill()
        for j in range(2):
            h = 2 * hp + j
            bc = eall[0][:, GLA_DK * h:GLA_DK * (h + 1)]
            blast = bc[q - 1:q, :]
            st = gla_ref[0, h]
            o = (_nn(att2[:, j * q:(j + 1) * q].astype(BF16), vs[j])
                 + _nt((qs[j] * jnp.exp(bc)).astype(BF16), st.astype(BF16)))
            kt = (ks_[j] * jnp.exp(blast - bc)).astype(BF16)
            gla_ref[0, h] = st * jnp.exp(blast) + _tn(vs[j], kt)
            o_parts.append(_rms(o))
    gr = main_ref[:, C_GR:C_GR + GLA_HEADS * GLA_DV].astype(F32)
    y_ref[0, :, SSD_INNER:SSD_INNER + GLA_HEADS * GLA_DV] = (
        jnp.concatenate(o_parts, axis=-1) * glanorm_ref[...] * _silu(gr)).astype(BF16)

    mm_old = mm_ref[0]
    mm_new = jnp.zeros((1, 128), F32)
    h_parts = []
    for h in range(ML_HEADS):
        qh = main_ref[:, C_MQ + ML_DK * h:C_MQ + ML_DK * (h + 1)]
        kh = (main_ref[:, C_MK + ML_DK * h:C_MK + ML_DK * (h + 1)].astype(F32) * (ML_DK ** -0.5)).astype(BF16)
        vh = main_ref[:, C_MV + ML_DV * h:C_MV + ML_DV * (h + 1)]
        fcc = cums[:, S_MF + h:S_MF + h + 1]
        fcr = tr[S_MF + h:S_MF + h + 1, 0:q]
        igr = tr[S_MI + h:S_MI + h + 1, q:2 * q]
        igc = pre[:, S_MI + h:S_MI + h + 1]
        m0 = mm_old[:, h:h + 1]
        dmat = jnp.where(causal, fcc - fcr + igr, neg_inf)
        inter = fcc + m0
        mt = jnp.maximum(inter, jnp.max(dmat, axis=1, keepdims=True))
        w = _nt(qh, kh) * jnp.exp(dmat - mt)
        wi = jnp.exp(inter - mt)
        c_old = mc_ref[0, h]
        n_old = mn_ref[0, h:h + 1, :]
        num = _nn(w.astype(BF16), vh) + wi * _nt(qh, c_old.astype(BF16))
        den = jnp.sum(w, axis=1, keepdims=True) + wi * jnp.sum(qh.astype(F32) * n_old, axis=1, keepdims=True)
        hb = num / jnp.maximum(jnp.abs(den), jnp.exp(-mt))
        m_new = mt[q - 1:q, :]
        flast = fcc[q - 1:q, :]
        wt = jnp.exp(flast - fcc + igc - m_new)
        dc = jnp.exp(flast + m0 - m_new)
        mc_ref[0, h] = c_old * dc + _tn((vh.astype(F32) * wt).astype(BF16), kh)
        mn_ref[0, h:h + 1, :] = n_old * dc + jnp.sum(wt * kh.astype(F32), axis=0, keepdims=True)
        mm_new = jnp.where(lane == h, m_new, mm_new)
        fill()
        og = _sigmoid(main_ref[:, C_MO + ML_DV * h:C_MO + ML_DV * (h + 1)].astype(F32)
                      + mlbo_ref[:, ML_DV * h:ML_DV * (h + 1)])
        h_parts.append(_rms(og * hb))
    mm_ref[0] = mm_new
    y_ref[0, :, 2 * D_MODEL:3 * D_MODEL] = (jnp.concatenate(h_parts, axis=-1) * mlnorm_ref[...]).astype(BF16)


def _mixers(x, mod3, st, lp, q):
    b, l, d = x.shape
    conv0, ssd0, gla0t, mc0, mn0, mm0 = st
    resident = lambda a: pl.BlockSpec(a.shape, lambda i, c: (0,) * a.ndim, pipeline_mode=pl.Buffered(1))
    n_chunks = l // q

    def nxt(i, c):
        n = jnp.minimum(i * n_chunks + c + 1, b * n_chunks - 1)
        return n // n_chunks, n % n_chunks
    nlev, wall, masks = _gla_level_constants(q)
    wall = jnp.asarray(wall, BF16)
    masks = jnp.asarray(np.concatenate([masks, masks], axis=-1), F32)
    shift = _conv_shift_constants(q)
    state_dims = [(CONV_ROWS, SSD_CONV_DIM), (SSD_GROUPS, SSD_STATE, SSD_INNER // SSD_GROUPS),
                  (GLA_HEADS, GLA_DV, GLA_DK),
                  (ML_HEADS, ML_DV, ML_DK), (ML_HEADS, ML_DK), (1, 128)]
    per_b = lambda shape, **kw: pl.BlockSpec((1,) + shape, lambda i, c: (i,) + (0,) * len(shape), **kw)
    shared = conv0.shape[0] == 1
    state_in = lambda shape: pl.BlockSpec(
        (1,) + shape, lambda i, c: ((0 if shared else i),) + (0,) * len(shape), pipeline_mode=pl.Buffered(1))
    consts = [lp['conv_w'], lp['conv_b'], lp['rowp'], lp['d_x'], lp['ssd_norm'], lp['w_gate'], lp['b_gate'],
              lp['gla_norm'], lp['ml_b_o'], lp['ml_norm'], wall, masks,
              jnp.asarray(shift, BF16)]
    state_shapes = [jax.ShapeDtypeStruct((b,) + s, F32) for s in state_dims]
    return pl.pallas_call(
        functools.partial(_mixer_kernel, q=q, nlev=nlev),
        grid=(b, l // q),
        in_specs=[pl.BlockSpec((1, q, d), lambda i, c: (0, 0, 0), pipeline_mode=pl.Buffered(1)),
                  pl.BlockSpec((1, 6, d), lambda i, c: (0, 0, 0), pipeline_mode=pl.Buffered(1)),
                  pl.BlockSpec((1, q, d), lambda i, c: nxt(i, c) + (0,)),
                  pl.BlockSpec((1, 6, d), lambda i, c: (nxt(i, c)[0], 0, 0)),
                  resident(lp['g_pre_mix']), resident(lp['w_main']), resident(lp['w_small'])]
        + [state_in(s) for s in state_dims] + [resident(a) for a in consts],
        out_specs=[pl.BlockSpec((1, q, 3 * D_MODEL), lambda i, c: (i, c, 0))] + [per_b(s) for s in state_dims],
        out_shape=[jax.ShapeDtypeStruct((b, l, 3 * D_MODEL), BF16)] + state_shapes,
        scratch_shapes=[pltpu.VMEM((CARRY_ROWS + q, SSD_CONV_DIM), BF16),
                        pltpu.VMEM((q, N_MAIN), BF16), pltpu.VMEM((q, N_MAIN), BF16),
                        pltpu.VMEM((q, N_SMALL), F32), pltpu.VMEM((q, N_SMALL), F32),
                        pltpu.VMEM((q, d), BF16)],
        compiler_params=_cparams(("arbitrary", "arbitrary")),
        name="mixers",
    )(x, mod3, x, mod3, lp['g_pre_mix'], lp['w_main'], lp['w_small'], conv0, ssd0, gla0t, mc0, mn0, mm0, *consts)


def _staggered(n, stages):
    for step in range(n + len(stages) - 1):
        for s in range(len(stages)):
            i = step - s
            if 0 <= i < n:
                stages[s](i)


def _sub_tiles(tb, tl):
    return tl // SUB_ROWS if (tb == 1 and tl % SUB_ROWS == 0) else 1


def _row_slice(sub, rows, i):
    return (slice(None),) * 3 if sub == 1 else (slice(None), slice(i * rows, (i + 1) * rows))


def _merge_kernel(x_ref, y_ref, mod_ref, gpre_ref, gpost_ref, wbr_ref, bbr_ref, wabc_ref, wout_ref, o_ref):
    tb, tl, d = x_ref.shape
    x = x_ref[...]
    h = _prenorm(x, gpre_ref[...], mod_ref[:, 0:1, :], mod_ref[:, 1:2, :])
    hb = h.reshape(tb * tl, d).astype(BF16)
    y = y_ref[...].reshape(tb * tl, 3 * d)
    merged = None
    for i in range(3):
        gate = _sigmoid(_nn(hb, wbr_ref[:, i * d:(i + 1) * d]) + bbr_ref[:, i * d:(i + 1) * d])
        t = gate * _nn(y[:, i * d:(i + 1) * d], wabc_ref[i])
        merged = t if merged is None else merged + t
    mix = _nn(merged.astype(BF16), wout_ref[...])
    o_ref[...] = x + mod_ref[:, 2:3, :] * (_rms(mix) * gpost_ref[...]).reshape(tb, tl, d)


def _merge(x, y, mod3, lp, tb, tl):
    b, l, d = x.shape
    const = lambda a: pl.BlockSpec(a.shape, lambda i, k: (0,) * a.ndim)
    consts = [lp['g_pre_mix'], lp['g_post_mix'], lp['w_br'], lp['b_br'], lp['w_abc'], lp['w_out']]
    return pl.pallas_call(
        _merge_kernel,
        grid=(b // tb, l // tl),
        in_specs=[pl.BlockSpec((tb, tl, d), lambda i, k: (i, k, 0)),
                  pl.BlockSpec((tb, tl, 3 * d), lambda i, k: (i, k, 0)),
                  pl.BlockSpec((tb, 6, d), lambda i, k: (i, 0, 0))] + [const(a) for a in consts],
        out_specs=pl.BlockSpec((tb, tl, d), lambda i, k: (i, k, 0)),
        out_shape=jax.ShapeDtypeStruct((b, l, d), F32),
        compiler_params=_cparams(("parallel", "parallel")),
        name="merge",
    )(x, y, mod3, *consts)


def _ffn_kernel(x_ref, mod_ref, gpre_ref, gpost_ref, wg_ref, wu_ref, wd_ref, o_ref, hb_ref, f_ref, *, chunks, sub):
    tb, tl, d = x_ref.shape
    rows = tb * tl // sub
    rsl = functools.partial(_row_slice, sub, rows)

    def pre(i):
        h = _prenorm(x_ref[rsl(i)], gpre_ref[...], mod_ref[:, 3:4, :], mod_ref[:, 4:5, :])
        hb_ref[i] = h.reshape(rows, d).astype(BF16)

    def mlp(i):
        hb = hb_ref[i]
        f, start = None, 0
        for th in chunks:
            hs = slice(start, start + th)
            start += th
            a = _nn(hb, wg_ref[:, hs])
            u = _nn(hb, wu_ref[:, hs])
            t = _nn((_silu(a) * u).astype(BF16), wd_ref[hs, :])
            f = t if f is None else f + t
        f_ref[i] = f

    def post(i):
        y = (_rms(f_ref[i]) * gpost_ref[...]).reshape((tb, tl // sub, d) if sub > 1 else (tb, tl, d))
        o_ref[rsl(i)] = x_ref[rsl(i)] + mod_ref[:, 5:6, :] * y

    _staggered(sub, [pre, mlp, post])


def _ffn(x, mod3, lp, tb, tl, chunks):
    b, l, d = x.shape
    sub = _sub_tiles(tb, tl)
    rows = tb * tl // sub
    resident = lambda a: pl.BlockSpec(a.shape, lambda i, k: (0,) * a.ndim, pipeline_mode=pl.Buffered(1))
    return pl.pallas_call(
        functools.partial(_ffn_kernel, chunks=chunks, sub=sub),
        grid=(b // tb, l // tl),
        scratch_shapes=[pltpu.VMEM((sub, rows, d), BF16), pltpu.VMEM((sub, rows, d), F32)],
        in_specs=[pl.BlockSpec((tb, tl, d), lambda i, k: (i, k, 0)),
                  pl.BlockSpec((tb, 6, d), lambda i, k: (i, 0, 0)),
                  resident(lp['g_pre_ffn']), resident(lp['g_post_ffn']),
                  resident(lp['w_ffn_gate']), resident(lp['w_ffn_up']), resident(lp['w_ffn_down'])],
        out_specs=pl.BlockSpec((tb, tl, d), lambda i, k: (i, k, 0)),
        out_shape=jax.ShapeDtypeStruct((b, l, d), F32),
        compiler_params=_cparams(("parallel", "parallel")),
        name="ffn",
    )(x, mod3, lp['g_pre_ffn'], lp['g_post_ffn'], lp['w_ffn_gate'], lp['w_ffn_up'], lp['w_ffn_down'])


def _pack_layer(p, l):
    d = D_MODEL
    w_in = p['w_in'][l]
    cut = lambda a, n: w_in[:, a:a + n]
    w_main = jnp.concatenate([cut(_O_Z, 1024), cut(_O_XBC, 1536), cut(_O_GQ, 512), cut(_O_GK, 512),
                              cut(_O_GV, 1024), cut(_O_GR, 1024), cut(_O_MQ, 1024), cut(_O_MK, 1024),
                              cut(_O_MV, 1024), cut(_O_MO, 1024)], axis=1).astype(BF16)
    w_small = jnp.concatenate([cut(_O_DT, 16), cut(_O_GLR, 16), cut(_O_MI, 4), cut(_O_MF, 4),
                               jnp.zeros((d, N_SMALL - 40), F32)], axis=1).astype(BF16)
    z = lambda n: jnp.zeros((n,), F32)
    bias_row = jnp.concatenate([p['ssd_dt_bias'][l], z(16), p['ml_b_i'][l], p['ml_b_f'][l], z(N_SMALL - 40)])
    alog_row = jnp.concatenate([p['ssd_a_log'][l], z(N_SMALL - 16)])
    rowp = jnp.zeros((8, N_SMALL), F32).at[0].set(bias_row).at[1].set(alog_row)
    w_gate = jnp.zeros((N_SMALL, GLA_HEADS * GLA_DK), F32).at[S_GLR:S_GLR + GLA_RANK].set(p['gla_w_gate'][l])
    row = lambda a: a.reshape(1, -1)
    return dict(
        w_main=w_main, w_small=w_small,
        w_br=cut(_O_BR, 3 * d).astype(BF16), b_br=row(p['b_branch'][l]),
        w_abc=jnp.stack([p['w_br_ssd'][l], p['w_br_gla'][l], p['w_br_ml'][l]]).astype(BF16),
        w_out=p['w_out'][l].astype(BF16),
        w_ffn_gate=p['w_ffn_gate'][l].astype(BF16), w_ffn_up=p['w_ffn_up'][l].astype(BF16),
        w_ffn_down=p['w_ffn_down'][l].astype(BF16),
        g_pre_mix=p['g_pre_mix'][l].reshape(1, 1, d), g_post_mix=row(p['g_post_mix'][l]),
        g_pre_ffn=p['g_pre_ffn'][l].reshape(1, 1, d), g_post_ffn=row(p['g_post_ffn'][l]),
        conv_w=p['ssd_conv_w'][l], conv_b=row(p['ssd_conv_b'][l]), rowp=rowp,
        d_x=row(jnp.repeat(p['ssd_d'][l], SSD_HEAD_DIM)), ssd_norm=row(p['ssd_norm'][l]),
        w_gate=w_gate.astype(BF16), b_gate=row(p['gla_b_gate'][l]), gla_norm=row(p['gla_norm'][l]),
        ml_b_o=row(p['ml_b_o'][l]), ml_norm=row(p['ml_norm'][l]))


MIX_CHUNK = 128


def _chunk_len(l):
    return MIX_CHUNK if l % MIX_CHUNK == 0 else l


MERGE_ROWS = 512
FFN_ROWS = 1024
FFN_HIDDEN_CHUNKS = (768, 768, 768, 512)


def _tiles(b, l, rows):
    if l >= rows:
        return 1, rows
    return b, l


def _trunk(x, mod, states, packed, q):
    b, l, d = x.shape
    depth = len(packed)
    new = []
    for li in range(depth):
        lp = packed[li]
        mod3 = mod[li].reshape(b, 6, d)
        conv0, ssd0, gla0, mc0, mn0, mm0 = (s[li] for s in states)
        sb = conv0.shape[0]
        st = (jnp.pad(conv0, ((0, 0), (CONV_ROWS - (SSD_CONV - 1), 0), (0, 0))),
              jnp.swapaxes(ssd0.reshape(sb, SSD_GROUPS, SSD_INNER // SSD_GROUPS, SSD_STATE), -1, -2),
              jnp.swapaxes(gla0, -1, -2), mc0, mn0,
              jnp.pad(mm0, ((0, 0), (0, 128 - ML_HEADS))).reshape(sb, 1, 128))
        y, conv_n, ssd_n, gla_n, mc_n, mn_n, mm_n = _mixers(x, mod3, st, lp, q)
        x = _merge(x, y, mod3, lp, *_tiles(b, l, MERGE_ROWS))
        x = _ffn(x, mod3, lp, *_tiles(b, l, FFN_ROWS), FFN_HIDDEN_CHUNKS)
        new.append((conv_n[:, CONV_ROWS - (SSD_CONV - 1):], jnp.swapaxes(ssd_n, -1, -2).reshape(b, SSD_HEADS, SSD_HEAD_DIM, SSD_STATE),
                    jnp.swapaxes(gla_n, -1, -2),
                    mc_n, mn_n, mm_n[:, 0, :ML_HEADS]))
    return x, tuple(jnp.stack([st[i] for st in new]) for i in range(6))


def kernel(x_prompt, x_sample, c_prompt, c_sample, state_ssd_conv, state_ssd, state_gla, state_mlstm_c, state_mlstm_n, state_mlstm_m, w_ada, b_ada, g_pre_mix, g_post_mix, g_pre_ffn, g_post_ffn, w_in, ssd_conv_w, ssd_conv_b, ssd_dt_bias, ssd_a_log, ssd_d, ssd_norm, gla_w_gate, gla_b_gate, gla_norm, ml_b_i, ml_b_f, ml_b_o, ml_norm, b_branch, w_br_ssd, w_br_gla, w_br_ml, w_out, w_ffn_gate, w_ffn_up, w_ffn_down):
    params = dict(g_pre_mix=g_pre_mix, g_post_mix=g_post_mix, g_pre_ffn=g_pre_ffn, g_post_ffn=g_post_ffn,
                  w_in=w_in, ssd_conv_w=ssd_conv_w, ssd_conv_b=ssd_conv_b, ssd_dt_bias=ssd_dt_bias,
                  ssd_a_log=ssd_a_log, ssd_d=ssd_d, ssd_norm=ssd_norm, gla_w_gate=gla_w_gate,
                  gla_b_gate=gla_b_gate, gla_norm=gla_norm, ml_b_i=ml_b_i, ml_b_f=ml_b_f, ml_b_o=ml_b_o,
                  ml_norm=ml_norm, b_branch=b_branch, w_br_ssd=w_br_ssd, w_br_gla=w_br_gla, w_br_ml=w_br_ml,
                  w_out=w_out, w_ffn_gate=w_ffn_gate, w_ffn_up=w_ffn_up, w_ffn_down=w_ffn_down)
    depth = w_in.shape[0]
    packed = [_pack_layer(params, l) for l in range(depth)]
    bp, lp_ = x_prompt.shape[0], x_prompt.shape[1]
    bs, ls = x_sample.shape[0], x_sample.shape[1]

    mod = _ada(jnp.concatenate([c_prompt, c_sample], axis=0), w_ada.astype(BF16), b_ada)
    mod_p, mod_s = mod[:, :bp], mod[:, bp:]

    zeros = lambda *shape: jnp.zeros((depth, 1) + shape, F32)
    zero_states = (zeros(SSD_CONV - 1, SSD_CONV_DIM), zeros(SSD_HEADS, SSD_HEAD_DIM, SSD_STATE),
                   zeros(GLA_HEADS, GLA_DK, GLA_DV), zeros(ML_HEADS, ML_DV, ML_DK), zeros(ML_HEADS, ML_DK),
                   zeros(ML_HEADS))
    y_p, st_p = _trunk(x_prompt, mod_p, zero_states, packed, _chunk_len(lp_))
    sample_states = (state_ssd_conv, state_ssd, state_gla, state_mlstm_c, state_mlstm_n, state_mlstm_m)
    y_s, st_s = _trunk(x_sample, mod_s, sample_states, packed, _chunk_len(ls))
    return (y_p, y_s) + st_p + st_s
```

```python
import functools
import math

import numpy as np
import jax
import jax.numpy as jnp
from jax import lax
from jax.experimental import pallas as pl
from jax.experimental.pallas import tpu as pltpu

F32 = jnp.float32
BF16 = jnp.bfloat16

D_MODEL = 1024
EPS = 1e-6
SSD_HEADS = 16
SSD_HEAD_DIM = 64
SSD_INNER = 1024
SSD_GROUPS = 2
SSD_STATE = 128
SSD_CONV = 4
SSD_CONV_DIM = 1536
GLA_HEADS = 4
GLA_DK = 128
GLA_DV = 256
GLA_RANK = 16
GLA_TAU = 16.0
ML_HEADS = 4
ML_DK = 256
ML_DV = 256
ML_INNER = 1024
FFN_HIDDEN = 2816

_SPLITS = (1024, 1536, 16, 512, 512, 1024, 1024, 16, 1024, 1024, 1024, 4, 4, 1024, 3072)
_OFF = np.concatenate([[0], np.cumsum(_SPLITS)]).tolist()
(_O_Z, _O_XBC, _O_DT, _O_GQ, _O_GK, _O_GV, _O_GR, _O_GLR, _O_MQ, _O_MK, _O_MV, _O_MI, _O_MF, _O_MO,
 _O_BR, _O_END) = _OFF

C_Z, C_XBC, C_GQ, C_GK, C_GV, C_GR, C_MQ, C_MK, C_MV, C_MO, N_MAIN = (
    0, 1024, 2560, 3072, 3584, 4608, 5632, 6656, 7680, 8704, 9728)
S_DT, S_GLR, S_MI, S_MF = 0, 16, 32, 36
N_SMALL = 128

VMEM_LIMIT = 56 * 1024 * 1024


def _cparams(sem):
    return pltpu.CompilerParams(dimension_semantics=sem, vmem_limit_bytes=VMEM_LIMIT)


def _sigmoid(x):
    return 1.0 / (1.0 + jnp.exp(-x))


def _silu(x):
    return x * _sigmoid(x)


def _softplus_tail(x):
    return jnp.log(1.0 + jnp.exp(-jnp.abs(x)))


def _nn(a, b):
    return jnp.dot(a, b, preferred_element_type=F32)


def _nt(a, b):
    return lax.dot_general(a, b, (((1,), (1,)), ((), ())), preferred_element_type=F32)


def _tn(a, b):
    return lax.dot_general(a, b, (((0,), (0,)), ((), ())), preferred_element_type=F32)


def _split(x, terms):
    parts = []
    for i in range(terms):
        p = x.astype(BF16)
        parts.append(p)
        if i + 1 < terms:
            x = x - p.astype(F32)
    return parts


def _sel_left(w01, x, fn=_nn, terms=3):
    parts = _split(x, terms)
    out = fn(w01, parts[0])
    for p in parts[1:]:
        out = out + fn(w01, p)
    return out


def _rms(x):
    return x * lax.rsqrt(jnp.mean(x * x, axis=-1, keepdims=True) + EPS)


def _group_rms(y, groups):
    n = y.shape[-1] // groups
    return jnp.concatenate([_rms(y[:, i * n:(i + 1) * n]) for i in range(groups)], axis=-1)


def _ada_kernel(c_ref, w_ref, b_ref, o_ref):
    c = c_ref[...]
    o_ref[0] = _nn(_silu(c).astype(BF16), w_ref[0]) + b_ref[0]


def _ada(c_all, w_ada, b_ada):
    depth = w_ada.shape[0]
    n = c_all.shape[0]
    d = D_MODEL
    return pl.pallas_call(
        _ada_kernel,
        grid=(depth, 6),
        in_specs=[pl.BlockSpec((n, d), lambda l, j: (0, 0)),
                  pl.BlockSpec((1, d, d), lambda l, j: (l, 0, j)),
                  pl.BlockSpec((1, 1, d), lambda l, j: (l, 0, j))],
        out_specs=pl.BlockSpec((1, n, d), lambda l, j: (l, 0, j)),
        out_shape=jax.ShapeDtypeStruct((depth, n, 6 * d), F32),
        compiler_params=_cparams(("parallel", "parallel")),
        name="ada",
    )(c_all, w_ada, b_ada.reshape(depth, 1, 6 * d))


def _prenorm(x, g, sh, sc):
    return _rms(x) * g * (1.0 + sc) + sh


def _gla_level_constants(q):
    nlev = int(math.log2(q))
    assert 1 << nlev == q
    wall = np.zeros((nlev + 1, q, q), np.float32)
    masks = np.zeros((nlev + 1, q, q), np.float32)
    wall[0] = np.tril(np.ones((q, q), np.float32))
    for lev in range(nlev):
        seg = q >> lev
        half = seg // 2
        for t in range(q):
            pos = t % seg
            start = t - pos
            bnd = start + half - 1
            if pos >= half:
                wall[lev + 1, t, bnd + 1:t + 1] = 1.0
                masks[lev, t, start:start + half] = 1.0
            else:
                wall[lev + 1, t, t + 1:bnd + 1] = 1.0
    masks[nlev] = np.eye(q, dtype=np.float32)
    return nlev, wall.reshape((nlev + 1) * q, q), masks


CONV_ROWS = 16
HIST_ROWS = 8


PROJ_COLS = 512
GLA_SHORT_SUM = 8
SUB_ROWS = 256


def _projection_pieces(x_ref, mod_ref, gpre_ref, wmain_ref, wsmall_ref, main_ref, sm_ref, hb_ref):
    def head():
        hb = _prenorm(x_ref[...], gpre_ref[...], mod_ref[:, 0:1, :], mod_ref[:, 1:2, :])[0].astype(BF16)
        hb_ref[...] = hb
        sm_ref[...] = _nn(hb, wsmall_ref[...])

    def cols(j):
        cs = slice(j * PROJ_COLS, (j + 1) * PROJ_COLS)
        main_ref[:, cs] = _nn(hb_ref[...], wmain_ref[:, cs]).astype(BF16)

    return [head] + [functools.partial(cols, j) for j in range(N_MAIN // PROJ_COLS)]


class _Filler:
    def __init__(self, pieces):
        self._pieces = list(pieces)

    def __call__(self, count=1):
        for _ in range(count):
            if self._pieces:
                self._pieces.pop(0)()

    def drain(self):
        self(len(self._pieces))


def _mixer_kernel(x_ref, mod_ref, xn_ref, modn_ref, gpre_ref, wmain_ref, wsmall_ref,
                  conv0_ref, ssd0_ref, gla0_ref, mc0_ref, mn0_ref, mm0_ref,
                  convw_ref, convb_ref, rowp_ref, dx_ref, ssdnorm_ref, wgate_ref, bgate_ref, glanorm_ref,
                  mlbo_ref, mlnorm_ref, wall_ref, masks_ref,
                  y_ref, conv_ref, ssd_ref, gla_ref, mc_ref, mn_ref, mm_ref,
                  ext_ref, main0_ref, main1_ref, sm0_ref, sm1_ref, hb_ref, *, q, nlev):
    n = pl.program_id(0) * pl.num_programs(1) + pl.program_id(1)
    proj_w = (gpre_ref, wmain_ref, wsmall_ref)
    consts = (convw_ref, convb_ref, rowp_ref, dx_ref, ssdnorm_ref, wgate_ref, bgate_ref, glanorm_ref,
              mlbo_ref, mlnorm_ref, wall_ref, masks_ref)
    outs = (y_ref, conv_ref, ssd_ref, gla_ref, mc_ref, mn_ref, mm_ref, ext_ref)

    @pl.when(n == 0)
    def _first():
        _Filler(_projection_pieces(x_ref, mod_ref, *proj_w, main0_ref, sm0_ref, hb_ref)).drain()

    @pl.when(pl.program_id(1) > 0)
    def _carry():
        ext_ref[0:HIST_ROWS, :] = ext_ref[q:q + HIST_ROWS, :]

    @pl.when(pl.program_id(1) == 0)
    def _init():
        ext_ref[0:HIST_ROWS, :] = conv0_ref[0, CONV_ROWS - HIST_ROWS:CONV_ROWS, :]
        ssd_ref[...] = ssd0_ref[...]
        gla_ref[...] = gla0_ref[...]
        mc_ref[...] = mc0_ref[...]
        mn_ref[...] = mn0_ref[...]
        mm_ref[...] = mm0_ref[...]

    @pl.when(n % 2 == 0)
    def _even():
        fill = _Filler(_projection_pieces(xn_ref, modn_ref, *proj_w, main1_ref, sm1_ref, hb_ref))
        _mix_chunk(main0_ref, sm0_ref, *consts, *outs, fill, q=q, nlev=nlev)
        fill.drain()

    @pl.when(n % 2 == 1)
    def _odd():
        fill = _Filler(_projection_pieces(xn_ref, modn_ref, *proj_w, main0_ref, sm0_ref, hb_ref))
        _mix_chunk(main1_ref, sm1_ref, *consts, *outs, fill, q=q, nlev=nlev)
        fill.drain()


def _mix_chunk(main_ref, sm_ref,
               convw_ref, convb_ref, rowp_ref, dx_ref, ssdnorm_ref, wgate_ref, bgate_ref, glanorm_ref,
               mlbo_ref, mlnorm_ref, wall_ref, masks_ref,
               y_ref, conv_ref, ssd_ref, gla_ref, mc_ref, mn_ref, mm_ref, ext_ref, fill, *, q, nlev):
    sm = sm_ref[...]
    fill()
    row = lax.broadcasted_iota(jnp.int32, (q, q), 0)
    col = lax.broadcasted_iota(jnp.int32, (q, q), 1)
    causal = row >= col
    lane = lax.broadcasted_iota(jnp.int32, (1, 128), 1)
    lane_lo = lane < 64
    neg_inf = -jnp.inf

    smb = sm + rowp_ref[0:1, :]
    tail = _softplus_tail(smb)
    sp = jnp.maximum(smb, 0.0) + tail
    lsg = jnp.minimum(smb, 0.0) - tail
    a_row = -jnp.exp(rowp_ref[1:2, :])
    pre = jnp.where(lane < S_GLR, sp * a_row,
                    jnp.where(lane >= S_MF, jnp.where(lane < S_MF + ML_HEADS, lsg, 0.0),
                              jnp.where(lane >= S_MI, smb, 0.0)))
    tril = wall_ref[0:q, :]
    cums = _sel_left(tril, pre)
    eye = (lax.broadcasted_iota(jnp.int32, (128, 128), 0)
           == lax.broadcasted_iota(jnp.int32, (128, 128), 1)).astype(BF16)
    tr = _sel_left(eye, jnp.concatenate([cums, pre], axis=0), _nt)
    fill()

    xraw = main_ref[:, C_XBC:C_XBC + SSD_CONV_DIM].astype(F32)
    ext_ref[HIST_ROWS:HIST_ROWS + q, :] = xraw
    acc = convb_ref[...] + convw_ref[SSD_CONV - 1:SSD_CONV, :] * xraw
    for j in range(SSD_CONV - 1):
        lo = HIST_ROWS - (SSD_CONV - 1) + j
        acc = acc + convw_ref[j:j + 1, :] * ext_ref[lo:lo + q, :]
    conv_ref[0] = main_ref[q - CONV_ROWS:q, C_XBC:C_XBC + SSD_CONV_DIM].astype(F32)
    xc = _silu(acc)
    fill()
    xs = xc[:, :SSD_INNER]
    bm = xc[:, SSD_INNER:SSD_INNER + SSD_GROUPS * SSD_STATE]
    cm = xc[:, SSD_INNER + SSD_GROUPS * SSD_STATE:]

    is_dt = lane < S_GLR
    acum = jnp.where(is_dt, cums, 0.0)
    dt = jnp.where(is_dt, sp, 0.0)
    eac = jnp.exp(acum)
    dtt = dt * jnp.exp(acum[q - 1:q, :] - acum)

    def per_head(a):
        return jnp.concatenate([jnp.where(lane_lo, a[:, 2 * p:2 * p + 1], a[:, 2 * p + 1:2 * p + 2])
                                for p in range(SSD_HEADS // 2)], axis=1)

    eac_x = per_head(eac)
    xdt = xs * per_head(dt)
    xw = (xs * per_head(dtt)).astype(BF16)
    y_parts = []
    hg = SSD_INNER // SSD_GROUPS
    for g in range(SSD_GROUPS):
        gs = slice(g * hg, (g + 1) * hg)
        cg = cm[:, g * SSD_STATE:(g + 1) * SSD_STATE].astype(BF16)
        bg = bm[:, g * SSD_STATE:(g + 1) * SSD_STATE].astype(BF16)
        wg = _nt(cg, bg)
        s_old = ssd_ref[0, g]
        y_inter = eac_x[:, gs] * _nn(cg, s_old.astype(BF16))
        ssd_ref[0, g] = s_old * eac_x[q - 1:q, gs] + _tn(bg, xw[:, gs])
        for pl_ in range(4):
            p = g * 4 + pl_
            xp = xdt[:, 128 * p:128 * p + 128]
            a2, x2 = [], []
            for j in range(2):
                h = 2 * p + j
                dec = jnp.exp(jnp.where(causal, cums[:, h:h + 1] - tr[h:h + 1, 0:q], neg_inf))
                a2.append((wg * dec).astype(BF16))
                x2.append(jnp.where(lane_lo if j == 0 else jnp.logical_not(lane_lo), xp, 0.0).astype(BF16))
            yp = _nn(jnp.concatenate(a2, axis=1), jnp.concatenate(x2, axis=0))
            y_parts.append(yp + y_inter[:, 128 * pl_:128 * pl_ + 128])
            fill()
    y = jnp.concatenate(y_parts, axis=-1) + xs * dx_ref[...]
    y = y * _silu(main_ref[:, C_Z:C_Z + SSD_INNER].astype(F32))
    y_ref[0, :, 0:SSD_INNER] = (_group_rms(y, SSD_GROUPS) * ssdnorm_ref[...]).astype(BF16)

    la_pre = _nn(sm.astype(BF16), wgate_ref[...]) + bgate_ref[...]
    la = (jnp.minimum(la_pre, 0.0) - _softplus_tail(la_pre)) * (1.0 / GLA_TAU)
    n_long = 1 + sum(1 for lev in range(nlev) if (q >> lev) // 2 > GLA_SHORT_SUM)
    la_hi, la_lo = _split(la, 2)
    e_long = _nn(wall_ref[0:n_long * q, :], la_hi) + _nn(wall_ref[0:n_long * q, :], la_lo)
    eall = [e_long[i * q:(i + 1) * q] for i in range(n_long)]
    if n_long < nlev + 1:
        e_short = _nn(wall_ref[n_long * q:(nlev + 1) * q, :], la_hi)
        eall += [e_short[i * q:(i + 1) * q] for i in range(nlev + 1 - n_long)]
    zeros_k = jnp.zeros((q, GLA_DK), BF16)
    o_parts = []
    for hp in range(GLA_HEADS // 2):
        qs, ks_, vs = [], [], []
        for j in range(2):
            h = 2 * hp + j
            qs.append(main_ref[:, C_GQ + GLA_DK * h:C_GQ + GLA_DK * (h + 1)].astype(F32) * (GLA_DK ** -0.5))
            ks_.append(main_ref[:, C_GK + GLA_DK * h:C_GK + GLA_DK * (h + 1)].astype(F32))
            vs.append(main_ref[:, C_GV + GLA_DV * h:C_GV + GLA_DV * (h + 1)])
        pair = slice(2 * GLA_DK * hp, 2 * GLA_DK * (hp + 1))

        def scores(scale):
            sc = [None, None] if scale is None else [scale[:, 0:GLA_DK], scale[:, GLA_DK:2 * GLA_DK]]
            qq = [(qs[j] if sc[j] is None else qs[j] * sc[j]).astype(BF16) for j in range(2)]
            kk = [(ks_[j] if sc[j] is None else ks_[j] * sc[j]).astype(BF16) for j in range(2)]
            kblk = jnp.concatenate([jnp.concatenate([kk[0], zeros_k], axis=1),
                                    jnp.concatenate([zeros_k, kk[1]], axis=1)], axis=0)
            return _nt(jnp.concatenate(qq, axis=1), kblk)

        att2 = scores(None) * masks_ref[nlev]
        for lev in range(nlev):
            att2 = att2 + scores(jnp.exp(eall[lev + 1][:, pair])) * masks_ref[lev]
            if lev % 2 == 1:
                fill()
        for j in range(2):
            h = 2 * hp + j
            bc = eall[0][:, GLA_DK * h:GLA_DK * (h + 1)]
            blast = bc[q - 1:q, :]
            st = gla_ref[0, h]
            o = (_nn(att2[:, j * q:(j + 1) * q].astype(BF16), vs[j])
                 + _nt((qs[j] * jnp.exp(bc)).astype(BF16), st.astype(BF16)))
            kt = (ks_[j] * jnp.exp(blast - bc)).astype(BF16)
            gla_ref[0, h] = st * jnp.exp(blast) + _tn(vs[j], kt)
            o_parts.append(_rms(o))
    gr = main_ref[:, C_GR:C_GR + GLA_HEADS * GLA_DV].astype(F32)
    y_ref[0, :, SSD_INNER:SSD_INNER + GLA_HEADS * GLA_DV] = (
        jnp.concatenate(o_parts, axis=-1) * glanorm_ref[...] * _silu(gr)).astype(BF16)

    mm_old = mm_ref[0]
    mm_new = jnp.zeros((1, 128), F32)
    h_parts = []
    for h in range(ML_HEADS):
        qh = main_ref[:, C_MQ + ML_DK * h:C_MQ + ML_DK * (h + 1)]
        kh = (main_ref[:, C_MK + ML_DK * h:C_MK + ML_DK * (h + 1)].astype(F32) * (ML_DK ** -0.5)).astype(BF16)
        vh = main_ref[:, C_MV + ML_DV * h:C_MV + ML_DV * (h + 1)]
        fcc = cums[:, S_MF + h:S_MF + h + 1]
        fcr = tr[S_MF + h:S_MF + h + 1, 0:q]
        igr = tr[S_MI + h:S_MI + h + 1, q:2 * q]
        igc = pre[:, S_MI + h:S_MI + h + 1]
        m0 = mm_old[:, h:h + 1]
        dmat = jnp.where(causal, fcc - fcr + igr, neg_inf)
        inter = fcc + m0
        mt = jnp.maximum(inter, jnp.max(dmat, axis=1, keepdims=True))
        w = _nt(qh, kh) * jnp.exp(dmat - mt)
        wi = jnp.exp(inter - mt)
        c_old = mc_ref[0, h]
        n_old = mn_ref[0, h:h + 1, :]
        num = _nn(w.astype(BF16), vh) + wi * _nt(qh, c_old.astype(BF16))
        den = jnp.sum(w, axis=1, keepdims=True) + wi * jnp.sum(qh.astype(F32) * n_old, axis=1, keepdims=True)
        hb = num / jnp.maximum(jnp.abs(den), jnp.exp(-mt))
        m_new = mt[q - 1:q, :]
        flast = fcc[q - 1:q, :]
        wt = jnp.exp(flast - fcc + igc - m_new)
        dc = jnp.exp(flast + m0 - m_new)
        mc_ref[0, h] = c_old * dc + _tn((vh.astype(F32) * wt).astype(BF16), kh)
        mn_ref[0, h:h + 1, :] = n_old * dc + jnp.sum(wt * kh.astype(F32), axis=0, keepdims=True)
        mm_new = jnp.where(lane == h, m_new, mm_new)
        fill()
        og = _sigmoid(main_ref[:, C_MO + ML_DV * h:C_MO + ML_DV * (h + 1)].astype(F32)
                      + mlbo_ref[:, ML_DV * h:ML_DV * (h + 1)])
        h_parts.append(_rms(og * hb))
    mm_ref[0] = mm_new
    y_ref[0, :, 2 * D_MODEL:3 * D_MODEL] = (jnp.concatenate(h_parts, axis=-1) * mlnorm_ref[...]).astype(BF16)


def _mixers(x, mod3, st, lp, q):
    b, l, d = x.shape
    conv0, ssd0, gla0t, mc0, mn0, mm0 = st
    resident = lambda a: pl.BlockSpec(a.shape, lambda i, c: (0,) * a.ndim, pipeline_mode=pl.Buffered(1))
    n_chunks = l // q

    def nxt(i, c):
        n = jnp.minimum(i * n_chunks + c + 1, b * n_chunks - 1)
        return n // n_chunks, n % n_chunks
    nlev, wall, masks = _gla_level_constants(q)
    wall = jnp.asarray(wall, BF16)
    masks = jnp.asarray(np.concatenate([masks, masks], axis=-1), F32)
    state_dims = [(CONV_ROWS, SSD_CONV_DIM), (SSD_GROUPS, SSD_STATE, SSD_INNER // SSD_GROUPS),
                  (GLA_HEADS, GLA_DV, GLA_DK),
                  (ML_HEADS, ML_DV, ML_DK), (ML_HEADS, ML_DK), (1, 128)]
    per_b = lambda shape, **kw: pl.BlockSpec((1,) + shape, lambda i, c: (i,) + (0,) * len(shape), **kw)
    shared = conv0.shape[0] == 1
    state_in = lambda shape: pl.BlockSpec(
        (1,) + shape, lambda i, c: ((0 if shared else i),) + (0,) * len(shape), pipeline_mode=pl.Buffered(1))
    consts = [lp['conv_w'], lp['conv_b'], lp['rowp'], lp['d_x'], lp['ssd_norm'], lp['w_gate'], lp['b_gate'],
              lp['gla_norm'], lp['ml_b_o'], lp['ml_norm'], wall, masks]
    state_shapes = [jax.ShapeDtypeStruct((b,) + s, F32) for s in state_dims]
    return pl.pallas_call(
        functools.partial(_mixer_kernel, q=q, nlev=nlev),
        grid=(b, l // q),
        in_specs=[pl.BlockSpec((1, q, d), lambda i, c: (0, 0, 0), pipeline_mode=pl.Buffered(1)),
                  pl.BlockSpec((1, 6, d), lambda i, c: (0, 0, 0), pipeline_mode=pl.Buffered(1)),
                  pl.BlockSpec((1, q, d), lambda i, c: nxt(i, c) + (0,)),
                  pl.BlockSpec((1, 6, d), lambda i, c: (nxt(i, c)[0], 0, 0)),
                  resident(lp['g_pre_mix']), resident(lp['w_main']), resident(lp['w_small'])]
        + [state_in(s) for s in state_dims] + [resident(a) for a in consts],
        out_specs=[pl.BlockSpec((1, q, 3 * D_MODEL), lambda i, c: (i, c, 0))] + [per_b(s) for s in state_dims],
        out_shape=[jax.ShapeDtypeStruct((b, l, 3 * D_MODEL), BF16)] + state_shapes,
        scratch_shapes=[pltpu.VMEM((HIST_ROWS + q, SSD_CONV_DIM), F32),
                        pltpu.VMEM((q, N_MAIN), BF16), pltpu.VMEM((q, N_MAIN), BF16),
                        pltpu.VMEM((q, N_SMALL), F32), pltpu.VMEM((q, N_SMALL), F32),
                        pltpu.VMEM((q, d), BF16)],
        compiler_params=_cparams(("arbitrary", "arbitrary")),
        name="mixers",
    )(x, mod3, x, mod3, lp['g_pre_mix'], lp['w_main'], lp['w_small'], conv0, ssd0, gla0t, mc0, mn0, mm0, *consts)


def _staggered(n, stages):
    for step in range(n + len(stages) - 1):
        for s in range(len(stages)):
            i = step - s
            if 0 <= i < n:
                stages[s](i)


def _sub_tiles(tb, tl):
    return tl // SUB_ROWS if (tb == 1 and tl % SUB_ROWS == 0) else 1


def _row_slice(sub, rows, i):
    return (slice(None),) * 3 if sub == 1 else (slice(None), slice(i * rows, (i + 1) * rows))


def _merge_kernel(x_ref, y_ref, mod_ref, gpre_ref, gpost_ref, wbr_ref, bbr_ref, wabc_ref, wout_ref, o_ref):
    tb, tl, d = x_ref.shape
    x = x_ref[...]
    h = _prenorm(x, gpre_ref[...], mod_ref[:, 0:1, :], mod_ref[:, 1:2, :])
    hb = h.reshape(tb * tl, d).astype(BF16)
    y = y_ref[...].reshape(tb * tl, 3 * d)
    merged = None
    for i in range(3):
        gate = _sigmoid(_nn(hb, wbr_ref[:, i * d:(i + 1) * d]) + bbr_ref[:, i * d:(i + 1) * d])
        t = gate * _nn(y[:, i * d:(i + 1) * d], wabc_ref[i])
        merged = t if merged is None else merged + t
    mix = _nn(merged.astype(BF16), wout_ref[...])
    o_ref[...] = x + mod_ref[:, 2:3, :] * (_rms(mix) * gpost_ref[...]).reshape(tb, tl, d)


def _merge(x, y, mod3, lp, tb, tl):
    b, l, d = x.shape
    const = lambda a: pl.BlockSpec(a.shape, lambda i, k: (0,) * a.ndim)
    consts = [lp['g_pre_mix'], lp['g_post_mix'], lp['w_br'], lp['b_br'], lp['w_abc'], lp['w_out']]
    return pl.pallas_call(
        _merge_kernel,
        grid=(b // tb, l // tl),
        in_specs=[pl.BlockSpec((tb, tl, d), lambda i, k: (i, k, 0)),
                  pl.BlockSpec((tb, tl, 3 * d), lambda i, k: (i, k, 0)),
                  pl.BlockSpec((tb, 6, d), lambda i, k: (i, 0, 0))] + [const(a) for a in consts],
        out_specs=pl.BlockSpec((tb, tl, d), lambda i, k: (i, k, 0)),
        out_shape=jax.ShapeDtypeStruct((b, l, d), F32),
        compiler_params=_cparams(("parallel", "parallel")),
        name="merge",
    )(x, y, mod3, *consts)


def _ffn_kernel(x_ref, mod_ref, gpre_ref, gpost_ref, wg_ref, wu_ref, wd_ref, o_ref, hb_ref, f_ref, *, chunks, sub):
    tb, tl, d = x_ref.shape
    rows = tb * tl // sub
    rsl = functools.partial(_row_slice, sub, rows)

    def pre(i):
        h = _prenorm(x_ref[rsl(i)], gpre_ref[...], mod_ref[:, 3:4, :], mod_ref[:, 4:5, :])
        hb_ref[i] = h.reshape(rows, d).astype(BF16)

    def mlp(i):
        hb = hb_ref[i]
        f, start = None, 0
        for th in chunks:
            hs = slice(start, start + th)
            start += th
            a = _nn(hb, wg_ref[:, hs])
            u = _nn(hb, wu_ref[:, hs])
            t = _nn((_silu(a) * u).astype(BF16), wd_ref[hs, :])
            f = t if f is None else f + t
        f_ref[i] = f

    def post(i):
        y = (_rms(f_ref[i]) * gpost_ref[...]).reshape((tb, tl // sub, d) if sub > 1 else (tb, tl, d))
        o_ref[rsl(i)] = x_ref[rsl(i)] + mod_ref[:, 5:6, :] * y

    _staggered(sub, [pre, mlp, post])


def _ffn(x, mod3, lp, tb, tl, chunks):
    b, l, d = x.shape
    sub = _sub_tiles(tb, tl)
    rows = tb * tl // sub
    resident = lambda a: pl.BlockSpec(a.shape, lambda i, k: (0,) * a.ndim, pipeline_mode=pl.Buffered(1))
    return pl.pallas_call(
        functools.partial(_ffn_kernel, chunks=chunks, sub=sub),
        grid=(b // tb, l // tl),
        scratch_shapes=[pltpu.VMEM((sub, rows, d), BF16), pltpu.VMEM((sub, rows, d), F32)],
        in_specs=[pl.BlockSpec((tb, tl, d), lambda i, k: (i, k, 0)),
                  pl.BlockSpec((tb, 6, d), lambda i, k: (i, 0, 0)),
                  resident(lp['g_pre_ffn']), resident(lp['g_post_ffn']),
                  resident(lp['w_ffn_gate']), resident(lp['w_ffn_up']), resident(lp['w_ffn_down'])],
        out_specs=pl.BlockSpec((tb, tl, d), lambda i, k: (i, k, 0)),
        out_shape=jax.ShapeDtypeStruct((b, l, d), F32),
        compiler_params=_cparams(("parallel", "parallel")),
        name="ffn",
    )(x, mod3, lp['g_pre_ffn'], lp['g_post_ffn'], lp['w_ffn_gate'], lp['w_ffn_up'], lp['w_ffn_down'])


def _pack_layer(p, l):
    d = D_MODEL
    w_in = p['w_in'][l]
    cut = lambda a, n: w_in[:, a:a + n]
    w_main = jnp.concatenate([cut(_O_Z, 1024), cut(_O_XBC, 1536), cut(_O_GQ, 512), cut(_O_GK, 512),
                              cut(_O_GV, 1024), cut(_O_GR, 1024), cut(_O_MQ, 1024), cut(_O_MK, 1024),
                              cut(_O_MV, 1024), cut(_O_MO, 1024)], axis=1).astype(BF16)
    w_small = jnp.concatenate([cut(_O_DT, 16), cut(_O_GLR, 16), cut(_O_MI, 4), cut(_O_MF, 4),
                               jnp.zeros((d, N_SMALL - 40), F32)], axis=1).astype(BF16)
    z = lambda n: jnp.zeros((n,), F32)
    bias_row = jnp.concatenate([p['ssd_dt_bias'][l], z(16), p['ml_b_i'][l], p['ml_b_f'][l], z(N_SMALL - 40)])
    alog_row = jnp.concatenate([p['ssd_a_log'][l], z(N_SMALL - 16)])
    rowp = jnp.zeros((8, N_SMALL), F32).at[0].set(bias_row).at[1].set(alog_row)
    w_gate = jnp.zeros((N_SMALL, GLA_HEADS * GLA_DK), F32).at[S_GLR:S_GLR + GLA_RANK].set(p['gla_w_gate'][l])
    row = lambda a: a.reshape(1, -1)
    return dict(
        w_main=w_main, w_small=w_small,
        w_br=cut(_O_BR, 3 * d).astype(BF16), b_br=row(p['b_branch'][l]),
        w_abc=jnp.stack([p['w_br_ssd'][l], p['w_br_gla'][l], p['w_br_ml'][l]]).astype(BF16),
        w_out=p['w_out'][l].astype(BF16),
        w_ffn_gate=p['w_ffn_gate'][l].astype(BF16), w_ffn_up=p['w_ffn_up'][l].astype(BF16),
        w_ffn_down=p['w_ffn_down'][l].astype(BF16),
        g_pre_mix=p['g_pre_mix'][l].reshape(1, 1, d), g_post_mix=row(p['g_post_mix'][l]),
        g_pre_ffn=p['g_pre_ffn'][l].reshape(1, 1, d), g_post_ffn=row(p['g_post_ffn'][l]),
        conv_w=p['ssd_conv_w'][l], conv_b=row(p['ssd_conv_b'][l]), rowp=rowp,
        d_x=row(jnp.repeat(p['ssd_d'][l], SSD_HEAD_DIM)), ssd_norm=row(p['ssd_norm'][l]),
        w_gate=w_gate.astype(BF16), b_gate=row(p['gla_b_gate'][l]), gla_norm=row(p['gla_norm'][l]),
        ml_b_o=row(p['ml_b_o'][l]), ml_norm=row(p['ml_norm'][l]))


MIX_CHUNK = 128


def _chunk_len(l):
    return MIX_CHUNK if l % MIX_CHUNK == 0 else l


MERGE_ROWS = 512
FFN_ROWS = 1024
FFN_HIDDEN_CHUNKS = (768, 768, 768, 512)


def _tiles(b, l, rows):
    if l >= rows:
        return 1, rows
    return b, l


def _trunk(x, mod, states, packed, q):
    b, l, d = x.shape
    depth = len(packed)
    new = []
    for li in range(depth):
        lp = packed[li]
        mod3 = mod[li].reshape(b, 6, d)
        conv0, ssd0, gla0, mc0, mn0, mm0 = (s[li] for s in states)
        sb = conv0.shape[0]
        st = (jnp.pad(conv0, ((0, 0), (CONV_ROWS - (SSD_CONV - 1), 0), (0, 0))),
              jnp.swapaxes(ssd0.reshape(sb, SSD_GROUPS, SSD_INNER // SSD_GROUPS, SSD_STATE), -1, -2),
              jnp.swapaxes(gla0, -1, -2), mc0, mn0,
              jnp.pad(mm0, ((0, 0), (0, 128 - ML_HEADS))).reshape(sb, 1, 128))
        y, conv_n, ssd_n, gla_n, mc_n, mn_n, mm_n = _mixers(x, mod3, st, lp, q)
        x = _merge(x, y, mod3, lp, *_tiles(b, l, MERGE_ROWS))
        x = _ffn(x, mod3, lp, *_tiles(b, l, FFN_ROWS), FFN_HIDDEN_CHUNKS)
        new.append((conv_n[:, CONV_ROWS - (SSD_CONV - 1):], jnp.swapaxes(ssd_n, -1, -2).reshape(b, SSD_HEADS, SSD_HEAD_DIM, SSD_STATE),
                    jnp.swapaxes(gla_n, -1, -2),
                    mc_n, mn_n, mm_n[:, 0, :ML_HEADS]))
    return x, tuple(jnp.stack([st[i] for st in new]) for i in range(6))


def kernel(x_prompt, x_sample, c_prompt, c_sample, state_ssd_conv, state_ssd, state_gla, state_mlstm_c, state_mlstm_n, state_mlstm_m, w_ada, b_ada, g_pre_mix, g_post_mix, g_pre_ffn, g_post_ffn, w_in, ssd_conv_w, ssd_conv_b, ssd_dt_bias, ssd_a_log, ssd_d, ssd_norm, gla_w_gate, gla_b_gate, gla_norm, ml_b_i, ml_b_f, ml_b_o, ml_norm, b_branch, w_br_ssd, w_br_gla, w_br_ml, w_out, w_ffn_gate, w_ffn_up, w_ffn_down):
    params = dict(g_pre_mix=g_pre_mix, g_post_mix=g_post_mix, g_pre_ffn=g_pre_ffn, g_post_ffn=g_post_ffn,
                  w_in=w_in, ssd_conv_w=ssd_conv_w, ssd_conv_b=ssd_conv_b, ssd_dt_bias=ssd_dt_bias,
                  ssd_a_log=ssd_a_log, ssd_d=ssd_d, ssd_norm=ssd_norm, gla_w_gate=gla_w_gate,
                  gla_b_gate=gla_b_gate, gla_norm=gla_norm, ml_b_i=ml_b_i, ml_b_f=ml_b_f, ml_b_o=ml_b_o,
                  ml_norm=ml_norm, b_branch=b_branch, w_br_ssd=w_br_ssd, w_br_gla=w_br_gla, w_br_ml=w_br_ml,
                  w_out=w_out, w_ffn_gate=w_ffn_gate, w_ffn_up=w_ffn_up, w_ffn_down=w_ffn_down)
    depth = w_in.shape[0]
    packed = [_pack_layer(params, l) for l in range(depth)]
    bp, lp_ = x_prompt.shape[0], x_prompt.shape[1]
    bs, ls = x_sample.shape[0], x_sample.shape[1]

    mod = _ada(jnp.concatenate([c_prompt, c_sample], axis=0), w_ada.astype(BF16), b_ada)
    mod_p, mod_s = mod[:, :bp], mod[:, bp:]

    zeros = lambda *shape: jnp.zeros((depth, 1) + shape, F32)
    zero_states = (zeros(SSD_CONV - 1, SSD_CONV_DIM), zeros(SSD_HEADS, SSD_HEAD_DIM, SSD_STATE),
                   zeros(GLA_HEADS, GLA_DK, GLA_DV), zeros(ML_HEADS, ML_DV, ML_DK), zeros(ML_HEADS, ML_DK),
                   zeros(ML_HEADS))
    y_p, st_p = _trunk(x_prompt, mod_p, zero_states, packed, _chunk_len(lp_))
    sample_states = (state_ssd_conv, state_ssd, state_gla, state_mlstm_c, state_mlstm_n, state_mlstm_m)
    y_s, st_s = _trunk(x_sample, mod_s, sample_states, packed, _chunk_len(ls))
    return (y_p, y_s) + st_p + st_s
```

```python
import functools
import math

import numpy as np
import jax
import jax.numpy as jnp
from jax import lax
from jax.experimental import pallas as pl
from jax.experimental.pallas import tpu as pltpu

F32 = jnp.float32
BF16 = jnp.bfloat16

D_MODEL = 1024
EPS = 1e-6
SSD_HEADS = 16
SSD_HEAD_DIM = 64
SSD_INNER = 1024
SSD_GROUPS = 2
SSD_STATE = 128
SSD_CONV = 4
SSD_CONV_DIM = 1536
GLA_HEADS = 4
GLA_DK = 128
GLA_DV = 256
GLA_RANK = 16
GLA_TAU = 16.0
ML_HEADS = 4
ML_DK = 256
ML_DV = 256
ML_INNER = 1024
FFN_HIDDEN = 2816

_SPLITS = (1024, 1536, 16, 512, 512, 1024, 1024, 16, 1024, 1024, 1024, 4, 4, 1024, 3072)
_OFF = np.concatenate([[0], np.cumsum(_SPLITS)]).tolist()
(_O_Z, _O_XBC, _O_DT, _O_GQ, _O_GK, _O_GV, _O_GR, _O_GLR, _O_MQ, _O_MK, _O_MV, _O_MI, _O_MF, _O_MO,
 _O_BR, _O_END) = _OFF

C_Z, C_XBC, C_GQ, C_GK, C_GV, C_GR, C_MQ, C_MK, C_MV, C_MO, N_MAIN = (
    0, 1024, 2560, 3072, 3584, 4608, 5632, 6656, 7680, 8704, 9728)
S_DT, S_GLR, S_MI, S_MF = 0, 16, 32, 36
N_SMALL = 128

VMEM_LIMIT = 56 * 1024 * 1024


def _cparams(sem):
    return pltpu.CompilerParams(dimension_semantics=sem, vmem_limit_bytes=VMEM_LIMIT)


def _sigmoid(x):
    return 1.0 / (1.0 + jnp.exp(-x))


def _silu(x):
    return x * _sigmoid(x)


def _softplus_tail(x):
    return jnp.log(1.0 + jnp.exp(-jnp.abs(x)))


def _nn(a, b):
    return jnp.dot(a, b, preferred_element_type=F32)


def _nt(a, b):
    return lax.dot_general(a, b, (((1,), (1,)), ((), ())), preferred_element_type=F32)


def _tn(a, b):
    return lax.dot_general(a, b, (((0,), (0,)), ((), ())), preferred_element_type=F32)


def _split(x, terms):
    parts = []
    for i in range(terms):
        p = x.astype(BF16)
        parts.append(p)
        if i + 1 < terms:
            x = x - p.astype(F32)
    return parts


def _sel_left(w01, x, fn=_nn, terms=3):
    parts = _split(x, terms)
    out = fn(w01, parts[0])
    for p in parts[1:]:
        out = out + fn(w01, p)
    return out


def _rms(x):
    return x * lax.rsqrt(jnp.mean(x * x, axis=-1, keepdims=True) + EPS)


def _group_rms(y, groups):
    n = y.shape[-1] // groups
    return jnp.concatenate([_rms(y[:, i * n:(i + 1) * n]) for i in range(groups)], axis=-1)


def _ada_kernel(c_ref, w_ref, b_ref, o_ref):
    c = c_ref[...]
    o_ref[0] = _nn(_silu(c).astype(BF16), w_ref[0]) + b_ref[0]


def _ada(c_all, w_ada, b_ada):
    depth = w_ada.shape[0]
    n = c_all.shape[0]
    d = D_MODEL
    return pl.pallas_call(
        _ada_kernel,
        grid=(depth, 6),
        in_specs=[pl.BlockSpec((n, d), lambda l, j: (0, 0)),
                  pl.BlockSpec((1, d, d), lambda l, j: (l, 0, j)),
                  pl.BlockSpec((1, 1, d), lambda l, j: (l, 0, j))],
        out_specs=pl.BlockSpec((1, n, d), lambda l, j: (l, 0, j)),
        out_shape=jax.ShapeDtypeStruct((depth, n, 6 * d), F32),
        compiler_params=_cparams(("parallel", "parallel")),
        name="ada",
    )(c_all, w_ada, b_ada.reshape(depth, 1, 6 * d))


def _prenorm(x, g, sh, sc):
    return _rms(x) * g * (1.0 + sc) + sh


def _gla_level_constants(q):
    nlev = int(math.log2(q))
    assert 1 << nlev == q
    wall = np.zeros((nlev + 1, q, q), np.float32)
    masks = np.zeros((nlev + 1, q, q), np.float32)
    wall[0] = np.tril(np.ones((q, q), np.float32))
    for lev in range(nlev):
        seg = q >> lev
        half = seg // 2
        for t in range(q):
            pos = t % seg
            start = t - pos
            bnd = start + half - 1
            if pos >= half:
                wall[lev + 1, t, bnd + 1:t + 1] = 1.0
                masks[lev, t, start:start + half] = 1.0
            else:
                wall[lev + 1, t, t + 1:bnd + 1] = 1.0
    masks[nlev] = np.eye(q, dtype=np.float32)
    return nlev, wall.reshape((nlev + 1) * q, q), masks


CONV_ROWS = 16
HIST_ROWS = 8


PROJ_COLS = 512
SUBLANES = 8
SUB_ROWS = 256


def _projection_pieces(x_ref, mod_ref, gpre_ref, wmain_ref, wsmall_ref, main_ref, sm_ref, hb_ref):
    def head():
        hb = _prenorm(x_ref[...], gpre_ref[...], mod_ref[:, 0:1, :], mod_ref[:, 1:2, :])[0].astype(BF16)
        hb_ref[...] = hb
        sm_ref[...] = _nn(hb, wsmall_ref[...])

    def cols(j):
        cs = slice(j * PROJ_COLS, (j + 1) * PROJ_COLS)
        main_ref[:, cs] = _nn(hb_ref[...], wmain_ref[:, cs]).astype(BF16)

    return [head] + [functools.partial(cols, j) for j in range(N_MAIN // PROJ_COLS)]


class _Filler:
    def __init__(self, pieces):
        self._pieces = list(pieces)

    def __call__(self, count=1):
        for _ in range(count):
            if self._pieces:
                self._pieces.pop(0)()

    def drain(self):
        self(len(self._pieces))


def _mixer_kernel(x_ref, mod_ref, xn_ref, modn_ref, gpre_ref, wmain_ref, wsmall_ref,
                  conv0_ref, ssd0_ref, gla0_ref, mc0_ref, mn0_ref, mm0_ref,
                  convw_ref, convb_ref, rowp_ref, dx_ref, ssdnorm_ref, wgate_ref, bgate_ref, glanorm_ref,
                  mlbo_ref, mlnorm_ref, wall_ref, masks_ref,
                  y_ref, conv_ref, ssd_ref, gla_ref, mc_ref, mn_ref, mm_ref,
                  ext_ref, main0_ref, main1_ref, sm0_ref, sm1_ref, hb_ref, *, q, nlev):
    n = pl.program_id(0) * pl.num_programs(1) + pl.program_id(1)
    proj_w = (gpre_ref, wmain_ref, wsmall_ref)
    consts = (convw_ref, convb_ref, rowp_ref, dx_ref, ssdnorm_ref, wgate_ref, bgate_ref, glanorm_ref,
              mlbo_ref, mlnorm_ref, wall_ref, masks_ref)
    outs = (y_ref, conv_ref, ssd_ref, gla_ref, mc_ref, mn_ref, mm_ref, ext_ref)

    @pl.when(n == 0)
    def _first():
        _Filler(_projection_pieces(x_ref, mod_ref, *proj_w, main0_ref, sm0_ref, hb_ref)).drain()

    @pl.when(pl.program_id(1) > 0)
    def _carry():
        ext_ref[0:HIST_ROWS, :] = ext_ref[q:q + HIST_ROWS, :]

    @pl.when(pl.program_id(1) == 0)
    def _init():
        ext_ref[0:HIST_ROWS, :] = conv0_ref[0, CONV_ROWS - HIST_ROWS:CONV_ROWS, :]
        ssd_ref[...] = ssd0_ref[...]
        gla_ref[...] = gla0_ref[...]
        mc_ref[...] = mc0_ref[...]
        mn_ref[...] = mn0_ref[...]
        mm_ref[...] = mm0_ref[...]

    @pl.when(n % 2 == 0)
    def _even():
        fill = _Filler(_projection_pieces(xn_ref, modn_ref, *proj_w, main1_ref, sm1_ref, hb_ref))
        _mix_chunk(main0_ref, sm0_ref, *consts, *outs, fill, q=q, nlev=nlev)
        fill.drain()

    @pl.when(n % 2 == 1)
    def _odd():
        fill = _Filler(_projection_pieces(xn_ref, modn_ref, *proj_w, main0_ref, sm0_ref, hb_ref))
        _mix_chunk(main1_ref, sm1_ref, *consts, *outs, fill, q=q, nlev=nlev)
        fill.drain()


def _mix_chunk(main_ref, sm_ref,
               convw_ref, convb_ref, rowp_ref, dx_ref, ssdnorm_ref, wgate_ref, bgate_ref, glanorm_ref,
               mlbo_ref, mlnorm_ref, wall_ref, masks_ref,
               y_ref, conv_ref, ssd_ref, gla_ref, mc_ref, mn_ref, mm_ref, ext_ref, fill, *, q, nlev):
    sm = sm_ref[...]
    fill()
    row = lax.broadcasted_iota(jnp.int32, (q, q), 0)
    col = lax.broadcasted_iota(jnp.int32, (q, q), 1)
    causal = row >= col
    lane = lax.broadcasted_iota(jnp.int32, (1, 128), 1)
    lane_lo = lane < 64
    neg_inf = -jnp.inf

    smb = sm + rowp_ref[0:1, :]
    tail = _softplus_tail(smb)
    sp = jnp.maximum(smb, 0.0) + tail
    lsg = jnp.minimum(smb, 0.0) - tail
    a_row = -jnp.exp(rowp_ref[1:2, :])
    pre = jnp.where(lane < S_GLR, sp * a_row,
                    jnp.where(lane >= S_MF, jnp.where(lane < S_MF + ML_HEADS, lsg, 0.0),
                              jnp.where(lane >= S_MI, smb, 0.0)))
    tril = wall_ref[0:q, :]
    cums = _sel_left(tril, pre)
    eye = (lax.broadcasted_iota(jnp.int32, (128, 128), 0)
           == lax.broadcasted_iota(jnp.int32, (128, 128), 1)).astype(BF16)
    tr = _sel_left(eye, jnp.concatenate([cums, pre], axis=0), _nt)
    fill()

    xraw = main_ref[:, C_XBC:C_XBC + SSD_CONV_DIM].astype(F32)
    ext_ref[HIST_ROWS:HIST_ROWS + q, :] = xraw
    acc = convb_ref[...] + convw_ref[SSD_CONV - 1:SSD_CONV, :] * xraw
    for j in range(SSD_CONV - 1):
        lo = HIST_ROWS - (SSD_CONV - 1) + j
        acc = acc + convw_ref[j:j + 1, :] * ext_ref[lo:lo + q, :]
    conv_ref[0] = main_ref[q - CONV_ROWS:q, C_XBC:C_XBC + SSD_CONV_DIM].astype(F32)
    xc = _silu(acc)
    fill()
    xs = xc[:, :SSD_INNER]
    bm = xc[:, SSD_INNER:SSD_INNER + SSD_GROUPS * SSD_STATE]
    cm = xc[:, SSD_INNER + SSD_GROUPS * SSD_STATE:]

    is_dt = lane < S_GLR
    acum = jnp.where(is_dt, cums, 0.0)
    dt = jnp.where(is_dt, sp, 0.0)
    eac = jnp.exp(acum)
    dtt = dt * jnp.exp(acum[q - 1:q, :] - acum)

    def per_head(a):
        return jnp.concatenate([jnp.where(lane_lo, a[:, 2 * p:2 * p + 1], a[:, 2 * p + 1:2 * p + 2])
                                for p in range(SSD_HEADS // 2)], axis=1)

    eac_x = per_head(eac)
    xdt = xs * per_head(dt)
    xw = (xs * per_head(dtt)).astype(BF16)
    y_parts = []
    hg = SSD_INNER // SSD_GROUPS
    for g in range(SSD_GROUPS):
        gs = slice(g * hg, (g + 1) * hg)
        cg = cm[:, g * SSD_STATE:(g + 1) * SSD_STATE].astype(BF16)
        bg = bm[:, g * SSD_STATE:(g + 1) * SSD_STATE].astype(BF16)
        wg = _nt(cg, bg)
        s_old = ssd_ref[0, g]
        y_inter = eac_x[:, gs] * _nn(cg, s_old.astype(BF16))
        ssd_ref[0, g] = s_old * eac_x[q - 1:q, gs] + _tn(bg, xw[:, gs])
        for pl_ in range(4):
            p = g * 4 + pl_
            xp = xdt[:, 128 * p:128 * p + 128]
            a2, x2 = [], []
            for j in range(2):
                h = 2 * p + j
                dec = jnp.exp(jnp.where(causal, cums[:, h:h + 1] - tr[h:h + 1, 0:q], neg_inf))
                a2.append((wg * dec).astype(BF16))
                x2.append(jnp.where(lane_lo if j == 0 else jnp.logical_not(lane_lo), xp, 0.0).astype(BF16))
            yp = _nn(jnp.concatenate(a2, axis=1), jnp.concatenate(x2, axis=0))
            y_parts.append(yp + y_inter[:, 128 * pl_:128 * pl_ + 128])
            fill()
    y = jnp.concatenate(y_parts, axis=-1) + xs * dx_ref[...]
    y = y * _silu(main_ref[:, C_Z:C_Z + SSD_INNER].astype(F32))
    y_ref[0, :, 0:SSD_INNER] = (_group_rms(y, SSD_GROUPS) * ssdnorm_ref[...]).astype(BF16)

    la_pre = _nn(sm.astype(BF16), wgate_ref[...]) + bgate_ref[...]
    la = (jnp.minimum(la_pre, 0.0) - _softplus_tail(la_pre)) * (1.0 / GLA_TAU)
    la_hi, la_lo = _split(la, 2)
    bc_all = _nn(wall_ref[0:q, :], la_hi) + _nn(wall_ref[0:q, :], la_lo)
    eall = [bc_all]
    n_diff = sum(1 for lev in range(nlev) if (q >> lev) // 2 >= SUBLANES)
    for lev in range(n_diff):
        seg = q >> lev
        half = seg // 2
        pieces = []
        for start in range(0, q, seg):
            bnd = bc_all[start + half - 1:start + half, :]
            pieces.append(bnd - bc_all[start:start + half, :])
            pieces.append(bc_all[start + half:start + seg, :] - bnd)
        eall.append(jnp.concatenate(pieces, axis=0))
    if n_diff < nlev:
        e_short = _nn(wall_ref[(n_diff + 1) * q:(nlev + 1) * q, :], la_hi)
        eall += [e_short[i * q:(i + 1) * q] for i in range(nlev - n_diff)]
    zeros_k = jnp.zeros((q, GLA_DK), BF16)
    o_parts = []
    for hp in range(GLA_HEADS // 2):
        qs, ks_, vs = [], [], []
        for j in range(2):
            h = 2 * hp + j
            qs.append(main_ref[:, C_GQ + GLA_DK * h:C_GQ + GLA_DK * (h + 1)].astype(F32) * (GLA_DK ** -0.5))
            ks_.append(main_ref[:, C_GK + GLA_DK * h:C_GK + GLA_DK * (h + 1)].astype(F32))
            vs.append(main_ref[:, C_GV + GLA_DV * h:C_GV + GLA_DV * (h + 1)])
        pair = slice(2 * GLA_DK * hp, 2 * GLA_DK * (hp + 1))

        def scores(scale):
            sc = [None, None] if scale is None else [scale[:, 0:GLA_DK], scale[:, GLA_DK:2 * GLA_DK]]
            qq = [(qs[j] if sc[j] is None else qs[j] * sc[j]).astype(BF16) for j in range(2)]
            kk = [(ks_[j] if sc[j] is None else ks_[j] * sc[j]).astype(BF16) for j in range(2)]
            kblk = jnp.concatenate([jnp.concatenate([kk[0], zeros_k], axis=1),
                                    jnp.concatenate([zeros_k, kk[1]], axis=1)], axis=0)
            return _nt(jnp.concatenate(qq, axis=1), kblk)

        att2 = scores(None) * masks_ref[nlev]
        for lev in range(nlev):
            att2 = att2 + scores(jnp.exp(eall[lev + 1][:, pair])) * masks_ref[lev]
            if lev % 2 == 1:
                fill()
        for j in range(2):
            h = 2 * hp + j
            bc = eall[0][:, GLA_DK * h:GLA_DK * (h + 1)]
            blast = bc[q - 1:q, :]
            st = gla_ref[0, h]
            o = (_nn(att2[:, j * q:(j + 1) * q].astype(BF16), vs[j])
                 + _nt((qs[j] * jnp.exp(bc)).astype(BF16), st.astype(BF16)))
            kt = (ks_[j] * jnp.exp(blast - bc)).astype(BF16)
            gla_ref[0, h] = st * jnp.exp(blast) + _tn(vs[j], kt)
            o_parts.append(_rms(o))
    gr = main_ref[:, C_GR:C_GR + GLA_HEADS * GLA_DV].astype(F32)
    y_ref[0, :, SSD_INNER:SSD_INNER + GLA_HEADS * GLA_DV] = (
        jnp.concatenate(o_parts, axis=-1) * glanorm_ref[...] * _silu(gr)).astype(BF16)

    mm_old = mm_ref[0]
    mm_new = jnp.zeros((1, 128), F32)
    h_parts = []
    for h in range(ML_HEADS):
        qh = main_ref[:, C_MQ + ML_DK * h:C_MQ + ML_DK * (h + 1)]
        kh = (main_ref[:, C_MK + ML_DK * h:C_MK + ML_DK * (h + 1)].astype(F32) * (ML_DK ** -0.5)).astype(BF16)
        vh = main_ref[:, C_MV + ML_DV * h:C_MV + ML_DV * (h + 1)]
        fcc = cums[:, S_MF + h:S_MF + h + 1]
        fcr = tr[S_MF + h:S_MF + h + 1, 0:q]
        igr = tr[S_MI + h:S_MI + h + 1, q:2 * q]
        igc = pre[:, S_MI + h:S_MI + h + 1]
        m0 = mm_old[:, h:h + 1]
        dmat = jnp.where(causal, fcc - fcr + igr, neg_inf)
        inter = fcc + m0
        mt = jnp.maximum(inter, jnp.max(dmat, axis=1, keepdims=True))
        w = _nt(qh, kh) * jnp.exp(dmat - mt)
        wi = jnp.exp(inter - mt)
        c_old = mc_ref[0, h]
        n_old = mn_ref[0, h:h + 1, :]
        num = _nn(w.astype(BF16), vh) + wi * _nt(qh, c_old.astype(BF16))
        den = jnp.sum(w, axis=1, keepdims=True) + wi * jnp.sum(qh.astype(F32) * n_old, axis=1, keepdims=True)
        hb = num / jnp.maximum(jnp.abs(den), jnp.exp(-mt))
        m_new = mt[q - 1:q, :]
        flast = fcc[q - 1:q, :]
        wt = jnp.exp(flast - fcc + igc - m_new)
        dc = jnp.exp(flast + m0 - m_new)
        mc_ref[0, h] = c_old * dc + _tn((vh.astype(F32) * wt).astype(BF16), kh)
        mn_ref[0, h:h + 1, :] = n_old * dc + jnp.sum(wt * kh.astype(F32), axis=0, keepdims=True)
        mm_new = jnp.where(lane == h, m_new, mm_new)
        fill()
        og = _sigmoid(main_ref[:, C_MO + ML_DV * h:C_MO + ML_DV * (h + 1)].astype(F32)
                      + mlbo_ref[:, ML_DV * h:ML_DV * (h + 1)])
        h_parts.append(_rms(og * hb))
    mm_ref[0] = mm_new
    y_ref[0, :, 2 * D_MODEL:3 * D_MODEL] = (jnp.concatenate(h_parts, axis=-1) * mlnorm_ref[...]).astype(BF16)


def _mixers(x, mod3, st, lp, q):
    b, l, d = x.shape
    conv0, ssd0, gla0t, mc0, mn0, mm0 = st
    resident = lambda a: pl.BlockSpec(a.shape, lambda i, c: (0,) * a.ndim, pipeline_mode=pl.Buffered(1))
    n_chunks = l // q

    def nxt(i, c):
        n = jnp.minimum(i * n_chunks + c + 1, b * n_chunks - 1)
        return n // n_chunks, n % n_chunks
    nlev, wall, masks = _gla_level_constants(q)
    wall = jnp.asarray(wall, BF16)
    masks = jnp.asarray(np.concatenate([masks, masks], axis=-1), F32)
    state_dims = [(CONV_ROWS, SSD_CONV_DIM), (SSD_GROUPS, SSD_STATE, SSD_INNER // SSD_GROUPS),
                  (GLA_HEADS, GLA_DV, GLA_DK),
                  (ML_HEADS, ML_DV, ML_DK), (ML_HEADS, ML_DK), (1, 128)]
    per_b = lambda shape, **kw: pl.BlockSpec((1,) + shape, lambda i, c: (i,) + (0,) * len(shape), **kw)
    shared = conv0.shape[0] == 1
    state_in = lambda shape: pl.BlockSpec(
        (1,) + shape, lambda i, c: ((0 if shared else i),) + (0,) * len(shape), pipeline_mode=pl.Buffered(1))
    consts = [lp['conv_w'], lp['conv_b'], lp['rowp'], lp['d_x'], lp['ssd_norm'], lp['w_gate'], lp['b_gate'],
              lp['gla_norm'], lp['ml_b_o'], lp['ml_norm'], wall, masks]
    state_shapes = [jax.ShapeDtypeStruct((b,) + s, F32) for s in state_dims]
    return pl.pallas_call(
        functools.partial(_mixer_kernel, q=q, nlev=nlev),
        grid=(b, l // q),
        in_specs=[pl.BlockSpec((1, q, d), lambda i, c: (0, 0, 0), pipeline_mode=pl.Buffered(1)),
                  pl.BlockSpec((1, 6, d), lambda i, c: (0, 0, 0), pipeline_mode=pl.Buffered(1)),
                  pl.BlockSpec((1, q, d), lambda i, c: nxt(i, c) + (0,)),
                  pl.BlockSpec((1, 6, d), lambda i, c: (nxt(i, c)[0], 0, 0)),
                  resident(lp['g_pre_mix']), resident(lp['w_main']), resident(lp['w_small'])]
        + [state_in(s) for s in state_dims] + [resident(a) for a in consts],
        out_specs=[pl.BlockSpec((1, q, 3 * D_MODEL), lambda i, c: (i, c, 0))] + [per_b(s) for s in state_dims],
        out_shape=[jax.ShapeDtypeStruct((b, l, 3 * D_MODEL), BF16)] + state_shapes,
        scratch_shapes=[pltpu.VMEM((HIST_ROWS + q, SSD_CONV_DIM), F32),
                        pltpu.VMEM((q, N_MAIN), BF16), pltpu.VMEM((q, N_MAIN), BF16),
                        pltpu.VMEM((q, N_SMALL), F32), pltpu.VMEM((q, N_SMALL), F32),
                        pltpu.VMEM((q, d), BF16)],
        compiler_params=_cparams(("arbitrary", "arbitrary")),
        name="mixers",
    )(x, mod3, x, mod3, lp['g_pre_mix'], lp['w_main'], lp['w_small'], conv0, ssd0, gla0t, mc0, mn0, mm0, *consts)


def _staggered(n, stages):
    for step in range(n + len(stages) - 1):
        for s in range(len(stages)):
            i = step - s
            if 0 <= i < n:
                stages[s](i)


def _sub_tiles(tb, tl):
    return tl // SUB_ROWS if (tb == 1 and tl % SUB_ROWS == 0) else 1


def _row_slice(sub, rows, i):
    return (slice(None),) * 3 if sub == 1 else (slice(None), slice(i * rows, (i + 1) * rows))


def _merge_kernel(x_ref, y_ref, mod_ref, gpre_ref, gpost_ref, wbr_ref, bbr_ref, wabc_ref, wout_ref, o_ref):
    tb, tl, d = x_ref.shape
    x = x_ref[...]
    h = _prenorm(x, gpre_ref[...], mod_ref[:, 0:1, :], mod_ref[:, 1:2, :])
    hb = h.reshape(tb * tl, d).astype(BF16)
    y = y_ref[...].reshape(tb * tl, 3 * d)
    merged = None
    for i in range(3):
        gate = _sigmoid(_nn(hb, wbr_ref[:, i * d:(i + 1) * d]) + bbr_ref[:, i * d:(i + 1) * d])
        t = gate * _nn(y[:, i * d:(i + 1) * d], wabc_ref[i])
        merged = t if merged is None else merged + t
    mix = _nn(merged.astype(BF16), wout_ref[...])
    o_ref[...] = x + mod_ref[:, 2:3, :] * (_rms(mix) * gpost_ref[...]).reshape(tb, tl, d)


def _merge(x, y, mod3, lp, tb, tl):
    b, l, d = x.shape
    const = lambda a: pl.BlockSpec(a.shape, lambda i, k: (0,) * a.ndim)
    consts = [lp['g_pre_mix'], lp['g_post_mix'], lp['w_br'], lp['b_br'], lp['w_abc'], lp['w_out']]
    return pl.pallas_call(
        _merge_kernel,
        grid=(b // tb, l // tl),
        in_specs=[pl.BlockSpec((tb, tl, d), lambda i, k: (i, k, 0)),
                  pl.BlockSpec((tb, tl, 3 * d), lambda i, k: (i, k, 0)),
                  pl.BlockSpec((tb, 6, d), lambda i, k: (i, 0, 0))] + [const(a) for a in consts],
        out_specs=pl.BlockSpec((tb, tl, d), lambda i, k: (i, k, 0)),
        out_shape=jax.ShapeDtypeStruct((b, l, d), F32),
        compiler_params=_cparams(("parallel", "parallel")),
        name="merge",
    )(x, y, mod3, *consts)


def _ffn_kernel(x_ref, mod_ref, gpre_ref, gpost_ref, wg_ref, wu_ref, wd_ref, o_ref, hb_ref, f_ref, *, chunks, sub):
    tb, tl, d = x_ref.shape
    rows = tb * tl // sub
    rsl = functools.partial(_row_slice, sub, rows)

    def pre(i):
        h = _prenorm(x_ref[rsl(i)], gpre_ref[...], mod_ref[:, 3:4, :], mod_ref[:, 4:5, :])
        hb_ref[i] = h.reshape(rows, d).astype(BF16)

    def mlp(i):
        hb = hb_ref[i]
        f, start = None, 0
        for th in chunks:
            hs = slice(start, start + th)
            start += th
            a = _nn(hb, wg_ref[:, hs])
            u = _nn(hb, wu_ref[:, hs])
            t = _nn((_silu(a) * u).astype(BF16), wd_ref[hs, :])
            f = t if f is None else f + t
        f_ref[i] = f

    def post(i):
        y = (_rms(f_ref[i]) * gpost_ref[...]).reshape((tb, tl // sub, d) if sub > 1 else (tb, tl, d))
        o_ref[rsl(i)] = x_ref[rsl(i)] + mod_ref[:, 5:6, :] * y

    _staggered(sub, [pre, mlp, post])


def _ffn(x, mod3, lp, tb, tl, chunks):
    b, l, d = x.shape
    sub = _sub_tiles(tb, tl)
    rows = tb * tl // sub
    resident = lambda a: pl.BlockSpec(a.shape, lambda i, k: (0,) * a.ndim, pipeline_mode=pl.Buffered(1))
    return pl.pallas_call(
        functools.partial(_ffn_kernel, chunks=chunks, sub=sub),
        grid=(b // tb, l // tl),
        scratch_shapes=[pltpu.VMEM((sub, rows, d), BF16), pltpu.VMEM((sub, rows, d), F32)],
        in_specs=[pl.BlockSpec((tb, tl, d), lambda i, k: (i, k, 0)),
                  pl.BlockSpec((tb, 6, d), lambda i, k: (i, 0, 0)),
                  resident(lp['g_pre_ffn']), resident(lp['g_post_ffn']),
                  resident(lp['w_ffn_gate']), resident(lp['w_ffn_up']), resident(lp['w_ffn_down'])],
        out_specs=pl.BlockSpec((tb, tl, d), lambda i, k: (i, k, 0)),
        out_shape=jax.ShapeDtypeStruct((b, l, d), F32),
        compiler_params=_cparams(("parallel", "parallel")),
        name="ffn",
    )(x, mod3, lp['g_pre_ffn'], lp['g_post_ffn'], lp['w_ffn_gate'], lp['w_ffn_up'], lp['w_ffn_down'])


def _pack_layer(p, l):
    d = D_MODEL
    w_in = p['w_in'][l]
    cut = lambda a, n: w_in[:, a:a + n]
    w_main = jnp.concatenate([cut(_O_Z, 1024), cut(_O_XBC, 1536), cut(_O_GQ, 512), cut(_O_GK, 512),
                              cut(_O_GV, 1024), cut(_O_GR, 1024), cut(_O_MQ, 1024), cut(_O_MK, 1024),
                              cut(_O_MV, 1024), cut(_O_MO, 1024)], axis=1).astype(BF16)
    w_small = jnp.concatenate([cut(_O_DT, 16), cut(_O_GLR, 16), cut(_O_MI, 4), cut(_O_MF, 4),
                               jnp.zeros((d, N_SMALL - 40), F32)], axis=1).astype(BF16)
    z = lambda n: jnp.zeros((n,), F32)
    bias_row = jnp.concatenate([p['ssd_dt_bias'][l], z(16), p['ml_b_i'][l], p['ml_b_f'][l], z(N_SMALL - 40)])
    alog_row = jnp.concatenate([p['ssd_a_log'][l], z(N_SMALL - 16)])
    rowp = jnp.zeros((8, N_SMALL), F32).at[0].set(bias_row).at[1].set(alog_row)
    w_gate = jnp.zeros((N_SMALL, GLA_HEADS * GLA_DK), F32).at[S_GLR:S_GLR + GLA_RANK].set(p['gla_w_gate'][l])
    row = lambda a: a.reshape(1, -1)
    return dict(
        w_main=w_main, w_small=w_small,
        w_br=cut(_O_BR, 3 * d).astype(BF16), b_br=row(p['b_branch'][l]),
        w_abc=jnp.stack([p['w_br_ssd'][l], p['w_br_gla'][l], p['w_br_ml'][l]]).astype(BF16),
        w_out=p['w_out'][l].astype(BF16),
        w_ffn_gate=p['w_ffn_gate'][l].astype(BF16), w_ffn_up=p['w_ffn_up'][l].astype(BF16),
        w_ffn_down=p['w_ffn_down'][l].astype(BF16),
        g_pre_mix=p['g_pre_mix'][l].reshape(1, 1, d), g_post_mix=row(p['g_post_mix'][l]),
        g_pre_ffn=p['g_pre_ffn'][l].reshape(1, 1, d), g_post_ffn=row(p['g_post_ffn'][l]),
        conv_w=p['ssd_conv_w'][l], conv_b=row(p['ssd_conv_b'][l]), rowp=rowp,
        d_x=row(jnp.repeat(p['ssd_d'][l], SSD_HEAD_DIM)), ssd_norm=row(p['ssd_norm'][l]),
        w_gate=w_gate.astype(BF16), b_gate=row(p['gla_b_gate'][l]), gla_norm=row(p['gla_norm'][l]),
        ml_b_o=row(p['ml_b_o'][l]), ml_norm=row(p['ml_norm'][l]))


MIX_CHUNK = 128


def _chunk_len(l):
    return MIX_CHUNK if l % MIX_CHUNK == 0 else l


MERGE_ROWS = 512
FFN_ROWS = 1024
FFN_HIDDEN_CHUNKS = (768, 768, 768, 512)


def _tiles(b, l, rows):
    if l >= rows:
        return 1, rows
    return b, l


def _trunk(x, mod, states, packed, q):
    b, l, d = x.shape
    depth = len(packed)
    new = []
    for li in range(depth):
        lp = packed[li]
        mod3 = mod[li].reshape(b, 6, d)
        conv0, ssd0, gla0, mc0, mn0, mm0 = (s[li] for s in states)
        sb = conv0.shape[0]
        st = (jnp.pad(conv0, ((0, 0), (CONV_ROWS - (SSD_CONV - 1), 0), (0, 0))),
              jnp.swapaxes(ssd0.reshape(sb, SSD_GROUPS, SSD_INNER // SSD_GROUPS, SSD_STATE), -1, -2),
              jnp.swapaxes(gla0, -1, -2), mc0, mn0,
              jnp.pad(mm0, ((0, 0), (0, 128 - ML_HEADS))).reshape(sb, 1, 128))
        y, conv_n, ssd_n, gla_n, mc_n, mn_n, mm_n = _mixers(x, mod3, st, lp, q)
        x = _merge(x, y, mod3, lp, *_tiles(b, l, MERGE_ROWS))
        x = _ffn(x, mod3, lp, *_tiles(b, l, FFN_ROWS), FFN_HIDDEN_CHUNKS)
        new.append((conv_n[:, CONV_ROWS - (SSD_CONV - 1):], jnp.swapaxes(ssd_n, -1, -2).reshape(b, SSD_HEADS, SSD_HEAD_DIM, SSD_STATE),
                    jnp.swapaxes(gla_n, -1, -2),
                    mc_n, mn_n, mm_n[:, 0, :ML_HEADS]))
    return x, tuple(jnp.stack([st[i] for st in new]) for i in range(6))


def kernel(x_prompt, x_sample, c_prompt, c_sample, state_ssd_conv, state_ssd, state_gla, state_mlstm_c, state_mlstm_n, state_mlstm_m, w_ada, b_ada, g_pre_mix, g_post_mix, g_pre_ffn, g_post_ffn, w_in, ssd_conv_w, ssd_conv_b, ssd_dt_bias, ssd_a_log, ssd_d, ssd_norm, gla_w_gate, gla_b_gate, gla_norm, ml_b_i, ml_b_f, ml_b_o, ml_norm, b_branch, w_br_ssd, w_br_gla, w_br_ml, w_out, w_ffn_gate, w_ffn_up, w_ffn_down):
    params = dict(g_pre_mix=g_pre_mix, g_post_mix=g_post_mix, g_pre_ffn=g_pre_ffn, g_post_ffn=g_post_ffn,
                  w_in=w_in, ssd_conv_w=ssd_conv_w, ssd_conv_b=ssd_conv_b, ssd_dt_bias=ssd_dt_bias,
                  ssd_a_log=ssd_a_log, ssd_d=ssd_d, ssd_norm=ssd_norm, gla_w_gate=gla_w_gate,
                  gla_b_gate=gla_b_gate, gla_norm=gla_norm, ml_b_i=ml_b_i, ml_b_f=ml_b_f, ml_b_o=ml_b_o,
                  ml_norm=ml_norm, b_branch=b_branch, w_br_ssd=w_br_ssd, w_br_gla=w_br_gla, w_br_ml=w_br_ml,
                  w_out=w_out, w_ffn_gate=w_ffn_gate, w_ffn_up=w_ffn_up, w_ffn_down=w_ffn_down)
    depth = w_in.shape[0]
    packed = [_pack_layer(params, l) for l in range(depth)]
    bp, lp_ = x_prompt.shape[0], x_prompt.shape[1]
    bs, ls = x_sample.shape[0], x_sample.shape[1]

    mod = _ada(jnp.concatenate([c_prompt, c_sample], axis=0), w_ada.astype(BF16), b_ada)
    mod_p, mod_s = mod[:, :bp], mod[:, bp:]

    zeros = lambda *shape: jnp.zeros((depth, 1) + shape, F32)
    zero_states = (zeros(SSD_CONV - 1, SSD_CONV_DIM), zeros(SSD_HEADS, SSD_HEAD_DIM, SSD_STATE),
                   zeros(GLA_HEADS, GLA_DK, GLA_DV), zeros(ML_HEADS, ML_DV, ML_DK), zeros(ML_HEADS, ML_DK),
                   zeros(ML_HEADS))
    y_p, st_p = _trunk(x_prompt, mod_p, zero_states, packed, _chunk_len(lp_))
    sample_states = (state_ssd_conv, state_ssd, state_gla, state_mlstm_c, state_mlstm_n, state_mlstm_m)
    y_s, st_s = _trunk(x_sample, mod_s, sample_states, packed, _chunk_len(ls))
    return (y_p, y_s) + st_p + st_s
```
